```python
import math
import jax, jax.numpy as jnp
from jax import lax
import numpy as np

D_MODEL = 1024
BATCH = 8
SEQ = 2048
DEPTH = 1
DEC_BATCH = 128
DEC_SEQ = 4
PAST_LEN = 16384
PAGE_SIZE = 128

D_MIX = D_MODEL
D_GLA = D_MIX // 2
D_CONV = D_MIX - D_GLA
GLA_HEADS = 4
GLA_DV = D_GLA // GLA_HEADS
GLA_DK = GLA_DV // 2
D_GLA_K = GLA_HEADS * GLA_DK
GATE_RANK = 16
GATE_NORMALIZER = 16.0
GLA_CHUNK = 64
CONV_WIDTH = 3
NORM_EPS = 1e-6

SPLIT_SIZES = [D_GLA_K, D_GLA_K, D_GLA, D_GLA, GATE_RANK, D_CONV, D_CONV, D_CONV, D_CONV]
N_IN = sum(SPLIT_SIZES)
SPLIT_IDX = np.cumsum(SPLIT_SIZES)[:-1].tolist()

kernel_name = "hymba_gla_shortconv_decode_step"


def rmsnorm(x, g):
    xf = x.astype(jnp.float32)
    y = xf * lax.rsqrt(jnp.mean(xf * xf, axis=-1, keepdims=True) + NORM_EPS)
    return (y * g.astype(jnp.float32)).astype(x.dtype)


def gla_chunked(q, k, v, gk, S0):
    B, L, H, DK = q.shape
    DV = v.shape[-1]
    C = math.gcd(L, GLA_CHUNK)
    N = L // C

    def to_chunks(t):
        return t.astype(jnp.float32).reshape(B, N, C, H, t.shape[-1]).transpose(1, 0, 3, 2, 4)

    mask = jnp.tril(jnp.ones((C, C), dtype=bool))

    def step(S, inp):
        qc, kc, vc, gc = inp
        b = jnp.cumsum(gc, axis=2)
        inter = jnp.einsum('bhtk,bhkv->bhtv', qc * jnp.exp(b), S)
        rel = b[:, :, :, None, :] - b[:, :, None, :, :]
        decay = jnp.exp(jnp.where(mask[:, :, None], rel, -jnp.inf))
        A = jnp.einsum('bhtk,bhtsk,bhsk->bhts', qc, decay, kc)
        intra = jnp.einsum('bhts,bhsv->bhtv', A, vc)
        b_last = b[:, :, -1:, :]
        S_new = jnp.exp(b_last[:, :, 0, :])[..., None] * S + jnp.einsum(
            'bhsk,bhsv->bhkv', kc * jnp.exp(b_last - b), vc)
        return S_new, inter + intra

    S_fin, o = lax.scan(step, S0.astype(jnp.float32),
                        (to_chunks(q), to_chunks(k), to_chunks(v), to_chunks(gk)))
    o = o.transpose(1, 0, 3, 2, 4).reshape(B, L, H, DV)
    return o, S_fin


def mixer_layer(x, S0, conv0, norm_gain, w_in, w_gk_up, b_gk, gla_norm_gain, conv_w, w_out):
    B, L, _ = x.shape
    h = rmsnorm(x, norm_gain)
    proj = jnp.einsum('bld,dn->bln', h, w_in)
    q, k, v, g_gla, gk_lr, u, b_gate, c_gate, g_conv = jnp.split(proj, SPLIT_IDX, axis=-1)

    q = q.reshape(B, L, GLA_HEADS, GLA_DK) * (GLA_DK ** -0.5)
    k = k.reshape(B, L, GLA_HEADS, GLA_DK)
    v = v.reshape(B, L, GLA_HEADS, GLA_DV)
    gk = jax.nn.log_sigmoid(
        (jnp.einsum('blr,rk->blk', gk_lr, w_gk_up) + b_gk).astype(jnp.float32)) / GATE_NORMALIZER
    gk = gk.reshape(B, L, GLA_HEADS, GLA_DK)
    o, S_new = gla_chunked(q, k, v, gk, S0)
    o = rmsnorm(o, gla_norm_gain).reshape(B, L, D_GLA).astype(x.dtype)
    o = o * jax.nn.silu(g_gla)

    hc = c_gate * u
    hp = jnp.concatenate([conv0.astype(hc.dtype), hc], axis=1)
    yc = sum(conv_w[j] * hp[:, j:j + L] for j in range(CONV_WIDTH))
    conv_new = hp[:, -(CONV_WIDTH - 1):]
    yc = b_gate * yc * jax.nn.silu(g_conv)

    mix = jnp.concatenate([o, yc], axis=-1)
    out = x + jnp.einsum('blm,md->bld', mix, w_out)
    return out, S_new.astype(S0.dtype), conv_new.astype(conv0.dtype)


def setup_inputs(seed: int = 0) -> dict:
    key = jax.random.key(seed)
    ks = jax.random.split(key, 12)
    f32 = jnp.float32
    return {
        "x_prompt": jax.random.normal(ks[0], (BATCH, SEQ, D_MODEL), f32),
        "x_sample": jax.random.normal(ks[1], (DEC_BATCH, DEC_SEQ, D_MODEL), f32),
        "state_gla": 0.3 * jax.random.normal(ks[2], (DEPTH, DEC_BATCH, GLA_HEADS, GLA_DK, GLA_DV), f32),
        "state_conv": jax.random.normal(ks[3], (DEPTH, DEC_BATCH, CONV_WIDTH - 1, D_CONV), f32),
        "norm_gain": 1.0 + 0.01 * jax.random.normal(ks[4], (DEPTH, D_MODEL), f32),
        "w_in": jax.random.normal(ks[5], (DEPTH, D_MODEL, N_IN), f32) * D_MODEL ** -0.5,
        "w_gk_up": jax.random.normal(ks[6], (DEPTH, GATE_RANK, D_GLA_K), f32) * GATE_RANK ** -0.5,
        "b_gk": 0.1 * jax.random.normal(ks[7], (DEPTH, D_GLA_K), f32),
        "gla_norm_gain": 1.0 + 0.01 * jax.random.normal(ks[8], (DEPTH, GLA_DV), f32),
        "conv_w": jax.random.normal(ks[9], (DEPTH, CONV_WIDTH, D_CONV), f32) * CONV_WIDTH ** -0.5,
        "w_out": jax.random.normal(ks[10], (DEPTH, D_MIX, D_MODEL), f32) * D_MIX ** -0.5,
        "final_norm_gain": 1.0 + 0.01 * jax.random.normal(ks[11], (D_MODEL,), f32),
    }


def reference(x_prompt, x_sample, state_gla, state_conv, norm_gain, w_in, w_gk_up, b_gk,
              gla_norm_gain, conv_w, w_out, final_norm_gain):
    hp, hs = x_prompt, x_sample
    Bp = x_prompt.shape[0]
    gla_p, conv_p, gla_s, conv_s = [], [], [], []
    for l in range(DEPTH):
        params = (norm_gain[l], w_in[l], w_gk_up[l], b_gk[l], gla_norm_gain[l], conv_w[l], w_out[l])
        S0p = jnp.zeros((Bp, GLA_HEADS, GLA_DK, GLA_DV), state_gla.dtype)
        c0p = jnp.zeros((Bp, CONV_WIDTH - 1, D_CONV), state_conv.dtype)
        hp, Sp, cp = mixer_layer(hp, S0p, c0p, *params)
        hs, Ss, cs = mixer_layer(hs, state_gla[l], state_conv[l], *params)
        gla_p.append(Sp)
        conv_p.append(cp)
        gla_s.append(Ss)
        conv_s.append(cs)
    y_prompt = rmsnorm(hp, final_norm_gain)
    y_sample = rmsnorm(hs, final_norm_gain)
    return (y_prompt, y_sample, jnp.stack(gla_p), jnp.stack(conv_p), jnp.stack(gla_s), jnp.stack(conv_s))
```

```python
import functools

import jax
import jax.numpy as jnp
from jax import lax
from jax.experimental import pallas as pl
from jax.experimental.pallas import tpu as pltpu

D_MODEL = 1024
HEADS = 4
DK = 64
DV = 128
D_QK = HEADS * DK
D_GLA = HEADS * DV
D_CONV = 512
RANK = 16
CHUNK = 64
TILE = 256
DEC_LEN = 4
SEQ_BLK = 32
GRP = 16
EPS = 1e-6
Q_SCALE = DK ** -0.5
GATE_SCALE = 1.0 / 16.0
VMEM_LIMIT = 52 * 1024 * 1024

F32 = jnp.float32
BF16 = jnp.bfloat16


def _dot(a, b):
    return jnp.dot(a, b, preferred_element_type=F32)


def _dot_nt(a, b):
    return lax.dot_general(a, b, (((1,), (1,)), ((), ())), preferred_element_type=F32)


def _dot_tn(a, b):
    return lax.dot_general(a, b, (((0,), (0,)), ((), ())), preferred_element_type=F32)


def _rmsnorm(x, gain):
    ms = jnp.mean(x * x, axis=-1, keepdims=True)
    return x * lax.rsqrt(ms + EPS) * gain


def _silu(x):
    return x * (1.0 / (1.0 + jnp.exp(-x)))


def _log_sigmoid(z):
    return jnp.minimum(z, 0.0) - jnp.log1p(jnp.exp(-jnp.abs(z)))


def _iota(shape, dim):
    return lax.broadcasted_iota(jnp.int32, shape, dim)


def _masked_sum(mask_bf16, g):
    g1 = g.astype(BF16)
    r1 = g - g1.astype(F32)
    g2 = r1.astype(BF16)
    g3 = (r1 - g2.astype(F32)).astype(BF16)
    return _dot(mask_bf16, g1) + _dot(mask_bf16, g2) + _dot(mask_bf16, g3)


def _head_stack(q):
    lane_head = _iota(q.shape, 1) >> 6
    return jnp.concatenate([jnp.where(lane_head == h, q, 0.0) for h in range(HEADS)], axis=0)


def _gate_logits(h, wlr_ref, wup_ref, bgk_ref):
    lr = _dot(h, wlr_ref[...])
    z = _dot(lr.astype(BF16), wup_ref[...]) + bgk_ref[...]
    return _log_sigmoid(z) * GATE_SCALE


def _gla_epilogue(o, gate, gng_ref):
    outs = []
    for hd in range(HEADS):
        oh = o[:, hd * DV:(hd + 1) * DV]
        outs.append(_rmsnorm(oh, gng_ref[...]))
    return jnp.concatenate(outs, axis=1) * _silu(gate)


def _out_proj(x, o, yc, wo_ref, fg_ref):
    mix = jnp.concatenate([o, yc], axis=1).astype(BF16)
    out = x + _dot(mix, wo_ref[...])
    return _rmsnorm(out, fg_ref[...])


def _prompt_kernel(x_ref, ng_ref, wa_ref, wlr_ref, wc_ref, wup_ref, bgk_ref, gng_ref, cw_ref,
                   wo_ref, fg_ref,
                   y_ref, sout_ref, cout_ref,
                   s_ref, sbd_ref, tail_ref):
    step = pl.program_id(1)

    @pl.when(step == 0)
    def _():
        s_ref[...] = jnp.zeros_like(s_ref)
        sbd_ref[...] = jnp.zeros_like(sbd_ref)
        tail_ref[...] = jnp.zeros_like(tail_ref)

    x = x_ref[...]
    h = _rmsnorm(x, ng_ref[...]).astype(BF16)

    qk = _dot(h, wa_ref[:, 0:2 * D_QK])
    v = _dot(h, wa_ref[:, 2 * D_QK:2 * D_QK + D_GLA])
    gate = _dot(h, wa_ref[:, 2 * D_QK + D_GLA:2 * D_QK + 2 * D_GLA])
    g = _gate_logits(h, wlr_ref, wup_ref, bgk_ref)

    rt = _iota((TILE, TILE), 0)
    ct = _iota((TILE, TILE), 1)
    cmask = jnp.where(((rt >> 6) == (ct >> 6)) & (ct <= rt), 1.0, 0.0).astype(BF16)
    b = _masked_sum(cmask, g)

    ar = _iota((HEADS * CHUNK, CHUNK), 0) & (CHUNK - 1)
    ac = _iota((HEADS * CHUNK, CHUNK), 1)
    causal = ac <= ar

    o_chunks = []
    for c in range(TILE // CHUNK):
        r0 = c * CHUNK
        bc = b[r0:r0 + CHUNK]
        qc = qk[r0:r0 + CHUNK, 0:D_QK] * Q_SCALE
        kc = qk[r0:r0 + CHUNK, D_QK:2 * D_QK]
        vc = v[r0:r0 + CHUNK].astype(BF16)
        bmid = bc[CHUNK // 2:CHUNK // 2 + 1]
        blast = bc[CHUNK - 1:CHUNK]
        q_in = (qc * jnp.exp(bc)).astype(BF16)
        q_a = qc * jnp.exp(bc - bmid)
        k_a = (kc * jnp.exp(bmid - bc)).astype(BF16)
        k_d = (kc * jnp.exp(blast - bc)).astype(BF16)

        a_all = _dot_nt(_head_stack(q_a).astype(BF16), k_a)
        a_all = jnp.where(causal, a_all, 0.0).astype(BF16)
        inter = _dot(q_in, sbd_ref[...])
        intra = jnp.concatenate(
            [_dot(a_all[hd * CHUNK:(hd + 1) * CHUNK], vc[:, hd * DV:(hd + 1) * DV])
             for hd in range(HEADS)], axis=1)
        o_chunks.append(inter + intra)

        u_all = _dot_tn(k_d, vc)
        dec = jnp.transpose(jnp.broadcast_to(jnp.exp(blast), (DV, D_QK)))
        for hd in range(HEADS):
            rows = slice(hd * DK, (hd + 1) * DK)
            cols = slice(hd * DV, (hd + 1) * DV)
            s_new = s_ref[hd] * dec[rows] + u_all[rows, cols]
            s_ref[hd] = s_new
            sbd_ref[rows, cols] = s_new.astype(BF16)

    o = _gla_epilogue(jnp.concatenate(o_chunks, axis=0), gate, gng_ref)

    u = _dot(h, wc_ref[:, 0:D_CONV])
    bg = _dot(h, wc_ref[:, D_CONV:2 * D_CONV])
    cg = _dot(h, wc_ref[:, 2 * D_CONV:3 * D_CONV])
    gc = _dot(h, wc_ref[:, 3 * D_CONV:4 * D_CONV])
    hc = cg * u
    tail = tail_ref[...]
    row = _iota(hc.shape, 0)
    h1 = jnp.where(row == 0, tail[7:8], pltpu.roll(hc, 1, 0))
    h2 = jnp.where(row == 0, tail[6:7], jnp.where(row == 1, tail[7:8], pltpu.roll(hc, 2, 0)))
    cw = cw_ref[...]
    yc = cw[0:1] * h2 + cw[1:2] * h1 + cw[2:3] * hc
    yc = bg * yc * _silu(gc)
    tail_ref[...] = hc[TILE - 8:TILE]

    y_ref[...] = _out_proj(x, o, yc, wo_ref, fg_ref)

    @pl.when(step == pl.num_programs(1) - 1)
    def _():
        sout_ref[...] = s_ref[...]
        cout_ref[...] = hc[TILE - 2:TILE]


def _sample_kernel(x_ref, cprev_ref, sin_ref, ng_ref, wa_ref, wlr_ref, wc_ref, wup_ref, bgk_ref,
                   gng_ref, cw_ref, wo_ref, fg_ref,
                   y_ref, sout_ref, hc_ref):
    nt = SEQ_BLK * DEC_LEN
    gt = GRP * DEC_LEN
    x = x_ref[...]
    h = _rmsnorm(x, ng_ref[...]).astype(BF16)

    qk = _dot(h, wa_ref[:, 0:2 * D_QK])
    v = _dot(h, wa_ref[:, 2 * D_QK:2 * D_QK + D_GLA]).astype(BF16)
    gate = _dot(h, wa_ref[:, 2 * D_QK + D_GLA:2 * D_QK + 2 * D_GLA])
    g = _gate_logits(h, wlr_ref, wup_ref, bgk_ref)

    rt = _iota((nt, nt), 0)
    ct = _iota((nt, nt), 1)
    same = (rt >> 2) == (ct >> 2)
    cmask = jnp.where(same & (ct <= rt), 1.0, 0.0).astype(BF16)
    fmask = jnp.where(same, 1.0, 0.0).astype(BF16)
    b = _masked_sum(cmask, g)
    bl = _masked_sum(fmask, g)

    q = qk[:, 0:D_QK] * Q_SCALE
    k = qk[:, D_QK:2 * D_QK]
    q_in = (q * jnp.exp(b)).astype(BF16)
    k_a = (k * jnp.exp(-b)).astype(BF16)
    k_d = k * jnp.exp(bl - b)
    k_dt = jnp.transpose(k_d).astype(BF16)
    dec_t = jnp.transpose(jnp.exp(bl))

    ar = _iota((HEADS * gt, gt), 0) & (gt - 1)
    ac = _iota((HEADS * gt, gt), 1)
    amask = ((ar >> 2) == (ac >> 2)) & (ac <= ar)
    ir = _iota((HEADS * gt, GRP * D_QK), 0)
    ic = _iota((HEADS * gt, GRP * D_QK), 1)
    imask = (((ir & (gt - 1)) >> 2) == (ic >> 8)) & ((ir >> 6) == ((ic >> 6) & 3))

    o_groups = []
    for gi in range(SEQ_BLK // GRP):
        r0 = gi * gt
        qg = q_in[r0:r0 + gt]
        a_all = _dot_nt(_head_stack(qg), k_a[r0:r0 + gt])
        a_all = jnp.where(amask, a_all, 0.0).astype(BF16)
        s_cat = sin_ref[gi * GRP:(gi + 1) * GRP].reshape(GRP * D_QK, DV).astype(BF16)
        lhs = jnp.where(imask, jnp.tile(qg, (HEADS, GRP)), 0.0)
        inter = _dot(lhs, s_cat)
        outs = []
        for hd in range(HEADS):
            rows = slice(hd * gt, (hd + 1) * gt)
            intra = _dot(a_all[rows], v[r0:r0 + gt, hd * DV:(hd + 1) * DV])
            outs.append(inter[rows] + intra)
        o_groups.append(jnp.concatenate(outs, axis=1))
    o = _gla_epilogue(jnp.concatenate(o_groups, axis=0), gate, gng_ref)

    ur = _iota((SEQ_BLK * DK, nt), 0)
    uc = _iota((SEQ_BLK * DK, nt), 1)
    umask = (ur >> 6) == (uc >> 2)
    for hd in range(HEADS):
        kt = jnp.tile(k_dt[hd * DK:(hd + 1) * DK], (SEQ_BLK, 1))
        u_h = _dot(jnp.where(umask, kt, 0.0), v[:, hd * DV:(hd + 1) * DV])
        for j in range(SEQ_BLK):
            col = DEC_LEN * j
            dec = jnp.broadcast_to(dec_t[hd * DK:(hd + 1) * DK, col:col + 1], (DK, DV))
            sout_ref[j, hd] = sin_ref[j, hd] * dec + u_h[j * DK:(j + 1) * DK]

    u = _dot(h, wc_ref[:, 0:D_CONV])
    bg = _dot(h, wc_ref[:, D_CONV:2 * D_CONV])
    cg = _dot(h, wc_ref[:, 2 * D_CONV:3 * D_CONV])
    gc = _dot(h, wc_ref[:, 3 * D_CONV:4 * D_CONV])
    hc = cg * u
    cprev = cprev_ref[...]
    t_in_seq = _iota(hc.shape, 0) & (DEC_LEN - 1)
    h1 = jnp.where(t_in_seq == 0, pltpu.roll(cprev, nt - 1, 0), pltpu.roll(hc, 1, 0))
    h2 = jnp.where(t_in_seq < 2, cprev, pltpu.roll(hc, 2, 0))
    cw = cw_ref[...]
    yc = cw[0:1] * h2 + cw[1:2] * h1 + cw[2:3] * hc
    yc = bg * yc * _silu(gc)
    hc_ref[...] = hc

    y_ref[...] = _out_proj(x, o, yc, wo_ref, fg_ref)


def _const_spec(shape):
    return pl.BlockSpec(shape, lambda *_: (0,) * len(shape))


def _weight_specs():
    return [
        _const_spec((1, D_MODEL)),
        _const_spec((D_MODEL, 2 * D_QK + 2 * D_GLA)),
        _const_spec((D_MODEL, RANK)),
        _const_spec((D_MODEL, 4 * D_CONV)),
        _const_spec((RANK, D_QK)),
        _const_spec((1, D_QK)),
        _const_spec((1, DV)),
        _const_spec((3, D_CONV)),
        _const_spec((D_MODEL, D_MODEL)),
        _const_spec((1, D_MODEL)),
    ]


def kernel(x_prompt, x_sample, state_gla, state_conv, norm_gain, w_in, w_gk_up, b_gk,
           gla_norm_gain, conv_w, w_out, final_norm_gain):
    n_batch, seq_len, _ = x_prompt.shape
    n_dec = x_sample.shape[0]
    n_a = 2 * D_QK + 2 * D_GLA
    w = w_in[0]
    weights = (
        norm_gain.reshape(1, D_MODEL),
        w[:, :n_a].astype(BF16),
        w[:, n_a:n_a + RANK].astype(BF16),
        w[:, n_a + RANK:].astype(BF16),
        w_gk_up[0].astype(BF16),
        b_gk.reshape(1, D_QK),
        gla_norm_gain.reshape(1, DV),
        conv_w[0],
        w_out[0].astype(BF16),
        final_norm_gain.reshape(1, D_MODEL),
    )

    y_p, s_p, c_p = pl.pallas_call(
        _prompt_kernel,
        grid=(n_batch, seq_len // TILE),
        in_specs=[pl.BlockSpec((None, TILE, D_MODEL), lambda bi, i: (bi, i, 0))] + _weight_specs(),
        out_specs=[
            pl.BlockSpec((None, TILE, D_MODEL), lambda bi, i: (bi, i, 0)),
            pl.BlockSpec((None, None, HEADS, DK, DV), lambda bi, i: (0, bi, 0, 0, 0)),
            pl.BlockSpec((None, None, 2, D_CONV), lambda bi, i: (0, bi, 0, 0)),
        ],
        out_shape=[
            jax.ShapeDtypeStruct((n_batch, seq_len, D_MODEL), F32),
            jax.ShapeDtypeStruct((1, n_batch, HEADS, DK, DV), F32),
            jax.ShapeDtypeStruct((1, n_batch, 2, D_CONV), F32),
        ],
        scratch_shapes=[
            pltpu.VMEM((HEADS, DK, DV), F32),
            pltpu.VMEM((D_QK, D_GLA), BF16),
            pltpu.VMEM((8, D_CONV), F32),
        ],
        compiler_params=pltpu.CompilerParams(
            dimension_semantics=("arbitrary", "arbitrary"), vmem_limit_bytes=VMEM_LIMIT),
        name="gla_conv_prompt",
    )(x_prompt, *weights)

    nt = SEQ_BLK * DEC_LEN
    xs = x_sample.reshape(n_dec * DEC_LEN, D_MODEL)
    cprev = jnp.pad(state_conv[0], ((0, 0), (0, DEC_LEN - 2), (0, 0))).reshape(n_dec * DEC_LEN, D_CONV)
    y_s, s_s, hc_s = pl.pallas_call(
        _sample_kernel,
        grid=(n_dec // SEQ_BLK,),
        in_specs=[
            pl.BlockSpec((nt, D_MODEL), lambda i: (i, 0)),
            pl.BlockSpec((nt, D_CONV), lambda i: (i, 0)),
            pl.BlockSpec((None, SEQ_BLK, HEADS, DK, DV), lambda i: (0, i, 0, 0, 0)),
        ] + _weight_specs(),
        out_specs=[
            pl.BlockSpec((nt, D_MODEL), lambda i: (i, 0)),
            pl.BlockSpec((None, SEQ_BLK, HEADS, DK, DV), lambda i: (0, i, 0, 0, 0)),
            pl.BlockSpec((nt, D_CONV), lambda i: (i, 0)),
        ],
        out_shape=[
            jax.ShapeDtypeStruct((n_dec * DEC_LEN, D_MODEL), F32),
            jax.ShapeDtypeStruct((1, n_dec, HEADS, DK, DV), F32),
            jax.ShapeDtypeStruct((n_dec * DEC_LEN, D_CONV), F32),
        ],
        compiler_params=pltpu.CompilerParams(
            dimension_semantics=("arbitrary",), vmem_limit_bytes=VMEM_LIMIT),
        name="gla_conv_sample",
    )(xs, cprev, state_gla, *weights)

    y_sample = y_s.reshape(n_dec, DEC_LEN, D_MODEL)
    conv_sample = hc_s.reshape(n_dec, DEC_LEN, D_CONV)[:, DEC_LEN - 2:][None]
    return (y_p, y_sample, s_p, c_p, s_s, conv_sample)
```

```python
import functools

import jax
import jax.numpy as jnp
from jax import lax
from jax.experimental import pallas as pl
from jax.experimental.pallas import tpu as pltpu

D_MODEL = 1024
HEADS = 4
DK = 64
DV = 128
D_QK = HEADS * DK
D_GLA = HEADS * DV
D_CONV = 512
RANK = 16
CHUNK = 64
TILE = 256
DEC_LEN = 4
SEQ_BLK = 32
GRP = 16
EPS = 1e-6
Q_SCALE = DK ** -0.5
GATE_SCALE = 1.0 / 16.0
VMEM_LIMIT = 52 * 1024 * 1024

R_QK = 0
R_V = 2 * D_QK
R_GATE = R_V + D_GLA
R_CONV = R_GATE + D_GLA
R_LR = R_CONV + 4 * D_CONV
N_IN = R_LR + RANK

C_X = 0
C_QK = C_X + D_MODEL
C_GATE = C_QK + 2 * D_QK
C_CONV = C_GATE + D_GLA
C_LR = C_CONV + 4 * D_CONV
C_END = C_LR + 128

F32 = jnp.float32
BF16 = jnp.bfloat16


def _dot(a, b):
    return jnp.dot(a, b, preferred_element_type=F32)


def _proj(h, w_ref, lo, hi):
    return _dot(h, w_ref[:, lo:hi])


def _dot_nt(a, b):
    return lax.dot_general(a, b, (((1,), (1,)), ((), ())), preferred_element_type=F32)


def _dot_tn(a, b):
    return lax.dot_general(a, b, (((0,), (0,)), ((), ())), preferred_element_type=F32)


def _rmsnorm(x, gain):
    ms = jnp.mean(x * x, axis=-1, keepdims=True)
    return x * lax.rsqrt(ms + EPS) * gain


def _silu(x):
    return x * (1.0 / (1.0 + jnp.exp(-x)))


def _log_sigmoid(z):
    return jnp.minimum(z, 0.0) - jnp.log1p(jnp.exp(-jnp.abs(z)))


def _iota(shape, dim):
    return lax.broadcasted_iota(jnp.int32, shape, dim)


def _masked_sum(mask_bf16, g):
    g1 = g.astype(BF16)
    r1 = g - g1.astype(F32)
    g2 = r1.astype(BF16)
    g3 = (r1 - g2.astype(F32)).astype(BF16)
    return _dot(mask_bf16, g1) + _dot(mask_bf16, g2) + _dot(mask_bf16, g3)


def _head_stack(q):
    lane_head = _iota(q.shape, 1) >> 6
    return jnp.concatenate([jnp.where(lane_head == h, q, 0.0) for h in range(HEADS)], axis=0)


def _gate_log_decay(lr, wup_ref, bgk_ref):
    z = _dot(lr.astype(BF16), wup_ref[...]) + bgk_ref[...]
    return _log_sigmoid(z) * GATE_SCALE


def _gla_epilogue(o, gate, gng_ref):
    outs = []
    for hd in range(HEADS):
        oh = o[:, hd * DV:(hd + 1) * DV]
        outs.append(_rmsnorm(oh, gng_ref[...]))
    return jnp.concatenate(outs, axis=1) * _silu(gate)


def _out_proj(x, o, yc, wo_ref, fg_ref):
    mix = jnp.concatenate([o, yc], axis=1).astype(BF16)
    out = x + _dot(mix, wo_ref[...])
    return _rmsnorm(out, fg_ref[...])


ITEM_SCHEDULE = (2, 2, 1, 0, 1, 0, 1, 1, 1, 1, 1, 1, 2, 1)


def _project_items(x_ref, ng_ref, wt_ref, p_ref, pv_ref, slot):
    cache = {}

    def norm():
        x = x_ref[...]
        cache["h"] = _rmsnorm(x, ng_ref[...]).astype(BF16)
        p_ref[slot, :, C_X:C_QK] = x

    def to_p(row, col, width=256):
        def item():
            p_ref[slot, :, col:col + width] = _proj(cache["h"], wt_ref, row, row + width)
        return item

    def to_pv(off):
        def item():
            pv_ref[slot, :, off:off + 256] = _proj(
                cache["h"], wt_ref, R_V + off, R_V + off + 256).astype(BF16)
        return item

    items = [norm, to_p(R_LR, C_LR, RANK)]
    items += [to_p(R_QK + o, C_QK + o) for o in (0, 256)]
    items += [to_pv(o) for o in (0, 256)]
    items += [to_p(R_CONV + o, C_CONV + o) for o in range(0, 4 * D_CONV, 256)]
    items += [to_p(R_GATE + o, C_GATE + o) for o in (0, 256)]
    return items


def _finish_tile(p_ref, pv_ref, slot, wup_ref, bgk_ref, gng_ref, cw_ref, wo_ref, fg_ref,
                 y_ref, sout_ref, cout_ref, s_ref, sbd_ref, tail_ref, emit):
    g = _gate_log_decay(p_ref[slot, :, C_LR:C_LR + RANK], wup_ref, bgk_ref)
    emit()

    rt = _iota((TILE, TILE), 0)
    ct = _iota((TILE, TILE), 1)
    cmask = jnp.where(((rt >> 6) == (ct >> 6)) & (ct <= rt), 1.0, 0.0).astype(BF16)
    b = _masked_sum(cmask, g)
    emit()

    ar = _iota((HEADS * CHUNK, CHUNK), 0) & (CHUNK - 1)
    ac = _iota((HEADS * CHUNK, CHUNK), 1)
    causal = ac <= ar

    o_chunks = []
    for c in range(TILE // CHUNK):
        r0 = c * CHUNK
        bc = b[r0:r0 + CHUNK]
        qc = p_ref[slot, r0:r0 + CHUNK, C_QK:C_QK + D_QK] * Q_SCALE
        kc = p_ref[slot, r0:r0 + CHUNK, C_QK + D_QK:C_GATE]
        vc = pv_ref[slot, r0:r0 + CHUNK, :]
        bmid = bc[CHUNK // 2:CHUNK // 2 + 1]
        blast = bc[CHUNK - 1:CHUNK]
        q_in = (qc * jnp.exp(bc)).astype(BF16)
        q_a = qc * jnp.exp(bc - bmid)
        k_a = (kc * jnp.exp(bmid - bc)).astype(BF16)
        k_d = (kc * jnp.exp(blast - bc)).astype(BF16)

        a_all = _dot_nt(_head_stack(q_a).astype(BF16), k_a)
        a_all = jnp.where(causal, a_all, 0.0).astype(BF16)
        inter = _dot(q_in, sbd_ref[...])
        intra = jnp.concatenate(
            [_dot(a_all[hd * CHUNK:(hd + 1) * CHUNK], vc[:, hd * DV:(hd + 1) * DV])
             for hd in range(HEADS)], axis=1)
        o_chunks.append(inter + intra)

        u_all = _dot_tn(k_d, vc)
        dec = jnp.transpose(jnp.broadcast_to(jnp.exp(blast), (DV, D_QK)))
        for hd in range(HEADS):
            rows = slice(hd * DK, (hd + 1) * DK)
            cols = slice(hd * DV, (hd + 1) * DV)
            s_new = s_ref[hd] * dec[rows] + u_all[rows, cols]
            s_ref[hd] = s_new
            sbd_ref[rows, cols] = s_new.astype(BF16)
        emit()

    o = jnp.concatenate(o_chunks, axis=0)
    mix = []
    for hd in range(HEADS):
        cols = slice(hd * DV, (hd + 1) * DV)
        gate = p_ref[slot, :, C_GATE + hd * DV:C_GATE + (hd + 1) * DV]
        mix.append((_rmsnorm(o[:, cols], gng_ref[...]) * _silu(gate)).astype(BF16))
        emit()

    cw = cw_ref[...]
    row = _iota((TILE, 256), 0)
    for half in range(D_CONV // 256):
        cols = slice(half * 256, (half + 1) * 256)

        def conv_in(k, cols=cols):
            return p_ref[slot, :, C_CONV + k * D_CONV + cols.start:C_CONV + k * D_CONV + cols.stop]

        hc = conv_in(2) * conv_in(0)
        tail = tail_ref[:, cols]
        h1 = jnp.where(row == 0, tail[7:8], pltpu.roll(hc, 1, 0))
        h2 = jnp.where(row == 0, tail[6:7], jnp.where(row == 1, tail[7:8], pltpu.roll(hc, 2, 0)))
        yc = cw[0:1, cols] * h2 + cw[1:2, cols] * h1 + cw[2:3, cols] * hc
        mix.append((conv_in(1) * yc * _silu(conv_in(3))).astype(BF16))
        tail_ref[:, cols] = hc[TILE - 8:TILE]
        cout_ref[:, cols] = hc[TILE - 2:TILE]
        emit()

    out = p_ref[slot, :, C_X:C_QK] + _dot(jnp.concatenate(mix, axis=1), wo_ref[...])
    emit()
    half_rows = TILE // 2
    y_ref[0:half_rows] = _rmsnorm(out[0:half_rows], fg_ref[...])
    emit()
    y_ref[half_rows:TILE] = _rmsnorm(out[half_rows:TILE], fg_ref[...])
    sout_ref[...] = s_ref[...]


def _prompt_kernel(tiles_per_seq, x_ref, ng_ref, wt_ref, wup_ref, bgk_ref, gng_ref, cw_ref,
                   wo_ref, fg_ref,
                   y_ref, sout_ref, cout_ref,
                   p_ref, pv_ref, s_ref, sbd_ref, tail_ref):
    step = pl.program_id(0)

    @pl.when(step == 0)
    def _():
        p_ref[1] = jnp.zeros(p_ref.shape[1:], F32)
        pv_ref[1] = jnp.zeros(pv_ref.shape[1:], BF16)

    @pl.when(jnp.logical_or(step == 0, (step - 1) % tiles_per_seq == 0))
    def _():
        s_ref[...] = jnp.zeros_like(s_ref)
        sbd_ref[...] = jnp.zeros_like(sbd_ref)
        tail_ref[...] = jnp.zeros_like(tail_ref)

    def body(write_slot, read_slot):
        items = _project_items(x_ref, ng_ref, wt_ref, p_ref, pv_ref, write_slot)
        counts = iter(ITEM_SCHEDULE)
        items[0]()
        pending = iter(items[1:])

        def emit():
            for _ in range(next(counts)):
                next(pending)()

        _finish_tile(p_ref, pv_ref, read_slot, wup_ref, bgk_ref, gng_ref, cw_ref, wo_ref, fg_ref,
                     y_ref, sout_ref, cout_ref, s_ref, sbd_ref, tail_ref, emit)
        assert next(counts, None) is None and next(pending, None) is None

    @pl.when(step % 2 == 0)
    def _():
        body(0, 1)

    @pl.when(step % 2 == 1)
    def _():
        body(1, 0)


def _sample_kernel(x_ref, cprev_ref, sin_ref, ng_ref, wt_ref, wup_ref, bgk_ref,
                   gng_ref, cw_ref, wo_ref, fg_ref,
                   y_ref, sout_ref, hc_ref):
    nt = SEQ_BLK * DEC_LEN
    gt = GRP * DEC_LEN
    x = x_ref[...]
    h = _rmsnorm(x, ng_ref[...]).astype(BF16)

    qk = _proj(h, wt_ref, R_QK, R_V)
    v = _proj(h, wt_ref, R_V, R_GATE).astype(BF16)
    gate = _proj(h, wt_ref, R_GATE, R_CONV)
    g = _gate_log_decay(_proj(h, wt_ref, R_LR, N_IN), wup_ref, bgk_ref)

    rt = _iota((nt, nt), 0)
    ct = _iota((nt, nt), 1)
    same = (rt >> 2) == (ct >> 2)
    cmask = jnp.where(same & (ct <= rt), 1.0, 0.0).astype(BF16)
    fmask = jnp.where(same, 1.0, 0.0).astype(BF16)
    b = _masked_sum(cmask, g)
    bl = _masked_sum(fmask, g)

    q = qk[:, 0:D_QK] * Q_SCALE
    k = qk[:, D_QK:2 * D_QK]
    q_in = (q * jnp.exp(b)).astype(BF16)
    k_a = (k * jnp.exp(-b)).astype(BF16)
    k_d = k * jnp.exp(bl - b)
    k_dt = jnp.transpose(k_d).astype(BF16)
    dec_t = jnp.transpose(jnp.exp(bl))

    ar = _iota((HEADS * gt, gt), 0) & (gt - 1)
    ac = _iota((HEADS * gt, gt), 1)
    amask = ((ar >> 2) == (ac >> 2)) & (ac <= ar)
    ir = _iota((HEADS * gt, GRP * D_QK), 0)
    ic = _iota((HEADS * gt, GRP * D_QK), 1)
    imask = (((ir & (gt - 1)) >> 2) == (ic >> 8)) & ((ir >> 6) == ((ic >> 6) & 3))

    o_groups = []
    for gi in range(SEQ_BLK // GRP):
        r0 = gi * gt
        qg = q_in[r0:r0 + gt]
        a_all = _dot_nt(_head_stack(qg), k_a[r0:r0 + gt])
        a_all = jnp.where(amask, a_all, 0.0).astype(BF16)
        s_cat = sin_ref[gi * GRP:(gi + 1) * GRP].reshape(GRP * D_QK, DV).astype(BF16)
        lhs = jnp.where(imask, jnp.tile(qg, (HEADS, GRP)), 0.0)
        inter = _dot(lhs, s_cat)
        outs = []
        for hd in range(HEADS):
            rows = slice(hd * gt, (hd + 1) * gt)
            intra = _dot(a_all[rows], v[r0:r0 + gt, hd * DV:(hd + 1) * DV])
            outs.append(inter[rows] + intra)
        o_groups.append(jnp.concatenate(outs, axis=1))
    o = _gla_epilogue(jnp.concatenate(o_groups, axis=0), gate, gng_ref)

    ur = _iota((SEQ_BLK * DK, nt), 0)
    uc = _iota((SEQ_BLK * DK, nt), 1)
    umask = (ur >> 6) == (uc >> 2)
    for hd in range(HEADS):
        kt = jnp.tile(k_dt[hd * DK:(hd + 1) * DK], (SEQ_BLK, 1))
        u_h = _dot(jnp.where(umask, kt, 0.0), v[:, hd * DV:(hd + 1) * DV])
        for j in range(SEQ_BLK):
            col = DEC_LEN * j
            dec = jnp.broadcast_to(dec_t[hd * DK:(hd + 1) * DK, col:col + 1], (DK, DV))
            sout_ref[j, hd] = sin_ref[j, hd] * dec + u_h[j * DK:(j + 1) * DK]

    conv = _proj(h, wt_ref, R_CONV, R_LR)
    hc = conv[:, 2 * D_CONV:3 * D_CONV] * conv[:, 0:D_CONV]
    cprev = cprev_ref[...]
    t_in_seq = _iota(hc.shape, 0) & (DEC_LEN - 1)
    h1 = jnp.where(t_in_seq == 0, pltpu.roll(cprev, nt - 1, 0), pltpu.roll(hc, 1, 0))
    h2 = jnp.where(t_in_seq < 2, cprev, pltpu.roll(hc, 2, 0))
    cw = cw_ref[...]
    yc = cw[0:1] * h2 + cw[1:2] * h1 + cw[2:3] * hc
    yc = conv[:, D_CONV:2 * D_CONV] * yc * _silu(conv[:, 3 * D_CONV:4 * D_CONV])
    hc_ref[...] = hc

    y_ref[...] = _out_proj(x, o, yc, wo_ref, fg_ref)


def _const_spec(shape):
    return pl.BlockSpec(shape, lambda *_: (0,) * len(shape))


def _weight_specs():
    return [
        _const_spec((1, D_MODEL)),
        _const_spec((D_MODEL, N_IN)),
        _const_spec((RANK, D_QK)),
        _const_spec((1, D_QK)),
        _const_spec((1, DV)),
        _const_spec((3, D_CONV)),
        _const_spec((D_MODEL, D_MODEL)),
        _const_spec((1, D_MODEL)),
    ]


def kernel(x_prompt, x_sample, state_gla, state_conv, norm_gain, w_in, w_gk_up, b_gk,
           gla_norm_gain, conv_w, w_out, final_norm_gain):
    n_batch, seq_len, _ = x_prompt.shape
    n_dec = x_sample.shape[0]
    tiles_per_seq = seq_len // TILE
    n_tiles = n_batch * tiles_per_seq
    w = w_in[0]
    n_a = 2 * D_QK + 2 * D_GLA
    weights = (
        norm_gain.reshape(1, D_MODEL),
        jnp.concatenate([w[:, :n_a], w[:, n_a + RANK:], w[:, n_a:n_a + RANK]], axis=1).astype(BF16),
        w_gk_up[0].astype(BF16),
        b_gk.reshape(1, D_QK),
        gla_norm_gain.reshape(1, DV),
        conv_w[0],
        w_out[0].astype(BF16),
        final_norm_gain.reshape(1, D_MODEL),
    )

    def in_tile(j):
        t = jnp.minimum(j, n_tiles - 1)
        return (t // tiles_per_seq, t % tiles_per_seq, 0)

    def out_tile(j):
        t = jnp.maximum(j - 1, 0)
        return (t // tiles_per_seq, t % tiles_per_seq, 0)

    def out_seq(j):
        return jnp.maximum(j - 1, 0) // tiles_per_seq

    y_p, s_p, c_p = pl.pallas_call(
        functools.partial(_prompt_kernel, tiles_per_seq),
        grid=(n_tiles + 1,),
        in_specs=[pl.BlockSpec((None, TILE, D_MODEL), in_tile)] + _weight_specs(),
        out_specs=[
            pl.BlockSpec((None, TILE, D_MODEL), out_tile),
            pl.BlockSpec((None, None, HEADS, DK, DV), lambda j: (0, out_seq(j), 0, 0, 0)),
            pl.BlockSpec((None, None, 2, D_CONV), lambda j: (0, out_seq(j), 0, 0)),
        ],
        out_shape=[
            jax.ShapeDtypeStruct((n_batch, seq_len, D_MODEL), F32),
            jax.ShapeDtypeStruct((1, n_batch, HEADS, DK, DV), F32),
            jax.ShapeDtypeStruct((1, n_batch, 2, D_CONV), F32),
        ],
        scratch_shapes=[
            pltpu.VMEM((2, TILE, C_END), F32),
            pltpu.VMEM((2, TILE, D_GLA), BF16),
            pltpu.VMEM((HEADS, DK, DV), F32),
            pltpu.VMEM((D_QK, D_GLA), BF16),
            pltpu.VMEM((8, D_CONV), F32),
        ],
        compiler_params=pltpu.CompilerParams(
            dimension_semantics=("arbitrary",), vmem_limit_bytes=VMEM_LIMIT),
        name="gla_conv_prompt",
    )(x_prompt, *weights)

    nt = SEQ_BLK * DEC_LEN
    xs = x_sample.reshape(n_dec * DEC_LEN, D_MODEL)
    cprev = jnp.pad(state_conv[0], ((0, 0), (0, DEC_LEN - 2), (0, 0))).reshape(n_dec * DEC_LEN, D_CONV)
    y_s, s_s, hc_s = pl.pallas_call(
        _sample_kernel,
        grid=(n_dec // SEQ_BLK,),
        in_specs=[
            pl.BlockSpec((nt, D_MODEL), lambda i: (i, 0)),
            pl.BlockSpec((nt, D_CONV), lambda i: (i, 0)),
            pl.BlockSpec((None, SEQ_BLK, HEADS, DK, DV), lambda i: (0, i, 0, 0, 0)),
        ] + _weight_specs(),
        out_specs=[
            pl.BlockSpec((nt, D_MODEL), lambda i: (i, 0)),
            pl.BlockSpec((None, SEQ_BLK, HEADS, DK, DV), lambda i: (0, i, 0, 0, 0)),
            pl.BlockSpec((nt, D_CONV), lambda i: (i, 0)),
        ],
        out_shape=[
            jax.ShapeDtypeStruct((n_dec * DEC_LEN, D_MODEL), F32),
            jax.ShapeDtypeStruct((1, n_dec, HEADS, DK, DV), F32),
            jax.ShapeDtypeStruct((n_dec * DEC_LEN, D_CONV), F32),
        ],
        compiler_params=pltpu.CompilerParams(
            dimension_semantics=("arbitrary",), vmem_limit_bytes=VMEM_LIMIT),
        name="gla_conv_sample",
    )(xs, cprev, state_gla, *weights)

    y_sample = y_s.reshape(n_dec, DEC_LEN, D_MODEL)
    conv_sample = hc_s.reshape(n_dec, DEC_LEN, D_CONV)[:, DEC_LEN - 2:][None]
    return (y_p, y_sample, s_p, c_p, s_s, conv_sample)
```

```python
import functools

import jax
import jax.numpy as jnp
from jax import lax
from jax.experimental import pallas as pl
from jax.experimental.pallas import tpu as pltpu

D_MODEL = 1024
HEADS = 4
DK = 64
DV = 128
D_QK = HEADS * DK
D_GLA = HEADS * DV
D_CONV = 512
RANK = 16
CHUNK = 64
TILE = 256
DEC_LEN = 4
SEQ_BLK = 32
GRP = 16
EPS = 1e-6
Q_SCALE = DK ** -0.5
GATE_SCALE = 1.0 / 16.0
VMEM_LIMIT = 52 * 1024 * 1024

R_QK = 0
R_V = 2 * D_QK
R_GATE = R_V + D_GLA
R_LR = R_GATE + D_GLA
R_CONV = R_LR + RANK
N_IN = R_CONV + 4 * D_CONV

C_X = 0
C_QK = C_X + D_MODEL
C_GATE = C_QK + 2 * D_QK
C_CONV = C_GATE + D_GLA
C_LR = C_CONV + 4 * D_CONV
C_END = C_LR + 128

F32 = jnp.float32
BF16 = jnp.bfloat16


def _dot(a, b):
    return jnp.dot(a, b, preferred_element_type=F32)


def _dot_nt(a, b):
    return lax.dot_general(a, b, (((1,), (1,)), ((), ())), preferred_element_type=F32)


def _proj(h, wt_ref, lo, hi):
    return _dot_nt(h, wt_ref[lo:hi])


def _dot_tn(a, b):
    return lax.dot_general(a, b, (((0,), (0,)), ((), ())), preferred_element_type=F32)


def _rmsnorm(x, gain):
    ms = jnp.mean(x * x, axis=-1, keepdims=True)
    return x * lax.rsqrt(ms + EPS) * gain


def _silu(x):
    return x * (1.0 / (1.0 + jnp.exp(-x)))


def _log_sigmoid(z):
    return jnp.minimum(z, 0.0) - jnp.log1p(jnp.exp(-jnp.abs(z)))


def _iota(shape, dim):
    return lax.broadcasted_iota(jnp.int32, shape, dim)


def _masked_sum(mask_bf16, g):
    g1 = g.astype(BF16)
    r1 = g - g1.astype(F32)
    g2 = r1.astype(BF16)
    g3 = (r1 - g2.astype(F32)).astype(BF16)
    return _dot(mask_bf16, g1) + _dot(mask_bf16, g2) + _dot(mask_bf16, g3)


def _head_stack(q):
    lane_head = _iota(q.shape, 1) >> 6
    return jnp.concatenate([jnp.where(lane_head == h, q, 0.0) for h in range(HEADS)], axis=0)


def _gate_log_decay(lr, wup_ref, bgk_ref):
    z = _dot(lr.astype(BF16), wup_ref[...]) + bgk_ref[...]
    return _log_sigmoid(z) * GATE_SCALE


def _gla_epilogue(o, gate, gng_ref):
    outs = []
    for hd in range(HEADS):
        oh = o[:, hd * DV:(hd + 1) * DV]
        outs.append(_rmsnorm(oh, gng_ref[...]))
    return jnp.concatenate(outs, axis=1) * _silu(gate)


def _out_proj(x, o, yc, wo_ref, fg_ref):
    mix = jnp.concatenate([o, yc], axis=1).astype(BF16)
    out = x + _dot(mix, wo_ref[...])
    return _rmsnorm(out, fg_ref[...])


ITEM_SCHEDULE = (2, 2, 1, 0, 1, 0, 1, 1, 1, 1, 1, 1, 2, 1)


def _project_items(x_ref, ng_ref, wt_ref, p_ref, pv_ref, slot):
    cache = {}

    def norm():
        x = x_ref[...]
        cache["h"] = _rmsnorm(x, ng_ref[...]).astype(BF16)
        p_ref[slot, :, C_X:C_QK] = x

    def to_p(row, col, width=256):
        def item():
            p_ref[slot, :, col:col + width] = _proj(cache["h"], wt_ref, row, row + width)
        return item

    def to_pv(off):
        def item():
            pv_ref[slot, :, off:off + 256] = _proj(
                cache["h"], wt_ref, R_V + off, R_V + off + 256).astype(BF16)
        return item

    items = [norm, to_p(R_LR, C_LR, RANK)]
    items += [to_p(R_QK + o, C_QK + o) for o in (0, 256)]
    items += [to_pv(o) for o in (0, 256)]
    items += [to_p(R_CONV + o, C_CONV + o) for o in range(0, 4 * D_CONV, 256)]
    items += [to_p(R_GATE + o, C_GATE + o) for o in (0, 256)]
    return items


def _finish_tile(p_ref, pv_ref, slot, wup_ref, bgk_ref, gng_ref, cw_ref, wo_ref, fg_ref,
                 y_ref, sout_ref, cout_ref, s_ref, sbd_ref, tail_ref, emit):
    g = _gate_log_decay(p_ref[slot, :, C_LR:C_LR + RANK], wup_ref, bgk_ref)
    emit()

    rt = _iota((TILE, TILE), 0)
    ct = _iota((TILE, TILE), 1)
    cmask = jnp.where(((rt >> 6) == (ct >> 6)) & (ct <= rt), 1.0, 0.0).astype(BF16)
    b = _masked_sum(cmask, g)
    emit()

    ar = _iota((HEADS * CHUNK, CHUNK), 0) & (CHUNK - 1)
    ac = _iota((HEADS * CHUNK, CHUNK), 1)
    causal = ac <= ar

    o_chunks = []
    for c in range(TILE // CHUNK):
        r0 = c * CHUNK
        bc = b[r0:r0 + CHUNK]
        qc = p_ref[slot, r0:r0 + CHUNK, C_QK:C_QK + D_QK] * Q_SCALE
        kc = p_ref[slot, r0:r0 + CHUNK, C_QK + D_QK:C_GATE]
        vc = pv_ref[slot, r0:r0 + CHUNK, :]
        bmid = bc[CHUNK // 2:CHUNK // 2 + 1]
        blast = bc[CHUNK - 1:CHUNK]
        q_in = (qc * jnp.exp(bc)).astype(BF16)
        q_a = qc * jnp.exp(bc - bmid)
        k_a = (kc * jnp.exp(bmid - bc)).astype(BF16)
        k_d = (kc * jnp.exp(blast - bc)).astype(BF16)

        a_all = _dot_nt(_head_stack(q_a).astype(BF16), k_a)
        a_all = jnp.where(causal, a_all, 0.0).astype(BF16)
        inter = _dot(q_in, sbd_ref[...])
        intra = jnp.concatenate(
            [_dot(a_all[hd * CHUNK:(hd + 1) * CHUNK], vc[:, hd * DV:(hd + 1) * DV])
             for hd in range(HEADS)], axis=1)
        o_chunks.append(inter + intra)

        u_all = _dot_tn(k_d, vc)
        dec = jnp.transpose(jnp.broadcast_to(jnp.exp(blast), (DV, D_QK)))
        for hd in range(HEADS):
            rows = slice(hd * DK, (hd + 1) * DK)
            cols = slice(hd * DV, (hd + 1) * DV)
            s_new = s_ref[hd] * dec[rows] + u_all[rows, cols]
            s_ref[hd] = s_new
            sbd_ref[rows, cols] = s_new.astype(BF16)
        emit()

    o = jnp.concatenate(o_chunks, axis=0)
    mix = []
    for hd in range(HEADS):
        cols = slice(hd * DV, (hd + 1) * DV)
        gate = p_ref[slot, :, C_GATE + hd * DV:C_GATE + (hd + 1) * DV]
        mix.append((_rmsnorm(o[:, cols], gng_ref[...]) * _silu(gate)).astype(BF16))
        emit()

    cw = cw_ref[...]
    row = _iota((TILE, 256), 0)
    for half in range(D_CONV // 256):
        cols = slice(half * 256, (half + 1) * 256)

        def conv_in(k, cols=cols):
            return p_ref[slot, :, C_CONV + k * D_CONV + cols.start:C_CONV + k * D_CONV + cols.stop]

        hc = conv_in(2) * conv_in(0)
        tail = tail_ref[:, cols]
        h1 = jnp.where(row == 0, tail[7:8], pltpu.roll(hc, 1, 0))
        h2 = jnp.where(row == 0, tail[6:7], jnp.where(row == 1, tail[7:8], pltpu.roll(hc, 2, 0)))
        yc = cw[0:1, cols] * h2 + cw[1:2, cols] * h1 + cw[2:3, cols] * hc
        mix.append((conv_in(1) * yc * _silu(conv_in(3))).astype(BF16))
        tail_ref[:, cols] = hc[TILE - 8:TILE]
        cout_ref[:, cols] = hc[TILE - 2:TILE]
        emit()

    out = p_ref[slot, :, C_X:C_QK] + _dot(jnp.concatenate(mix, axis=1), wo_ref[...])
    emit()
    half_rows = TILE // 2
    y_ref[0:half_rows] = _rmsnorm(out[0:half_rows], fg_ref[...])
    emit()
    y_ref[half_rows:TILE] = _rmsnorm(out[half_rows:TILE], fg_ref[...])
    sout_ref[...] = s_ref[...]


def _prompt_kernel(tiles_per_seq, x_ref, ng_ref, wt_ref, wup_ref, bgk_ref, gng_ref, cw_ref,
                   wo_ref, fg_ref,
                   y_ref, sout_ref, cout_ref,
                   p_ref, pv_ref, s_ref, sbd_ref, tail_ref):
    step = pl.program_id(0)

    @pl.when(step == 0)
    def _():
        p_ref[1] = jnp.zeros(p_ref.shape[1:], F32)
        pv_ref[1] = jnp.zeros(pv_ref.shape[1:], BF16)

    @pl.when(jnp.logical_or(step == 0, (step - 1) % tiles_per_seq == 0))
    def _():
        s_ref[...] = jnp.zeros_like(s_ref)
        sbd_ref[...] = jnp.zeros_like(sbd_ref)
        tail_ref[...] = jnp.zeros_like(tail_ref)

    def body(write_slot, read_slot):
        items = _project_items(x_ref, ng_ref, wt_ref, p_ref, pv_ref, write_slot)
        counts = iter(ITEM_SCHEDULE)
        items[0]()
        pending = iter(items[1:])

        def emit():
            for _ in range(next(counts)):
                next(pending)()

        _finish_tile(p_ref, pv_ref, read_slot, wup_ref, bgk_ref, gng_ref, cw_ref, wo_ref, fg_ref,
                     y_ref, sout_ref, cout_ref, s_ref, sbd_ref, tail_ref, emit)
        assert next(counts, None) is None and next(pending, None) is None

    @pl.when(step % 2 == 0)
    def _():
        body(0, 1)

    @pl.when(step % 2 == 1)
    def _():
        body(1, 0)


def _sample_kernel(x_ref, cprev_ref, sin_ref, ng_ref, wt_ref, wup_ref, bgk_ref,
                   gng_ref, cw_ref, wo_ref, fg_ref,
                   y_ref, sout_ref, hc_ref):
    nt = SEQ_BLK * DEC_LEN
    gt = GRP * DEC_LEN
    x = x_ref[...]
    h = _rmsnorm(x, ng_ref[...]).astype(BF16)

    qk = _proj(h, wt_ref, R_QK, R_V)
    v = _proj(h, wt_ref, R_V, R_GATE).astype(BF16)
    gate = _proj(h, wt_ref, R_GATE, R_LR)
    g = _gate_log_decay(_proj(h, wt_ref, R_LR, R_CONV), wup_ref, bgk_ref)

    rt = _iota((nt, nt), 0)
    ct = _iota((nt, nt), 1)
    same = (rt >> 2) == (ct >> 2)
    cmask = jnp.where(same & (ct <= rt), 1.0, 0.0).astype(BF16)
    fmask = jnp.where(same, 1.0, 0.0).astype(BF16)
    b = _masked_sum(cmask, g)
    bl = _masked_sum(fmask, g)

    q = qk[:, 0:D_QK] * Q_SCALE
    k = qk[:, D_QK:2 * D_QK]
    q_in = (q * jnp.exp(b)).astype(BF16)
    k_a = (k * jnp.exp(-b)).astype(BF16)
    k_d = k * jnp.exp(bl - b)
    k_dt = jnp.transpose(k_d).astype(BF16)
    dec_t = jnp.transpose(jnp.exp(bl))

    ar = _iota((HEADS * gt, gt), 0) & (gt - 1)
    ac = _iota((HEADS * gt, gt), 1)
    amask = ((ar >> 2) == (ac >> 2)) & (ac <= ar)
    ir = _iota((HEADS * gt, GRP * D_QK), 0)
    ic = _iota((HEADS * gt, GRP * D_QK), 1)
    imask = (((ir & (gt - 1)) >> 2) == (ic >> 8)) & ((ir >> 6) == ((ic >> 6) & 3))

    o_groups = []
    for gi in range(SEQ_BLK // GRP):
        r0 = gi * gt
        qg = q_in[r0:r0 + gt]
        a_all = _dot_nt(_head_stack(qg), k_a[r0:r0 + gt])
        a_all = jnp.where(amask, a_all, 0.0).astype(BF16)
        s_cat = sin_ref[gi * GRP:(gi + 1) * GRP].reshape(GRP * D_QK, DV).astype(BF16)
        lhs = jnp.where(imask, jnp.tile(qg, (HEADS, GRP)), 0.0)
        inter = _dot(lhs, s_cat)
        outs = []
        for hd in range(HEADS):
            rows = slice(hd * gt, (hd + 1) * gt)
            intra = _dot(a_all[rows], v[r0:r0 + gt, hd * DV:(hd + 1) * DV])
            outs.append(inter[rows] + intra)
        o_groups.append(jnp.concatenate(outs, axis=1))
    o = _gla_epilogue(jnp.concatenate(o_groups, axis=0), gate, gng_ref)

    ur = _iota((SEQ_BLK * DK, nt), 0)
    uc = _iota((SEQ_BLK * DK, nt), 1)
    umask = (ur >> 6) == (uc >> 2)
    for hd in range(HEADS):
        kt = jnp.tile(k_dt[hd * DK:(hd + 1) * DK], (SEQ_BLK, 1))
        u_h = _dot(jnp.where(umask, kt, 0.0), v[:, hd * DV:(hd + 1) * DV])
        for j in range(SEQ_BLK):
            col = DEC_LEN * j
            dec = jnp.broadcast_to(dec_t[hd * DK:(hd + 1) * DK, col:col + 1], (DK, DV))
            sout_ref[j, hd] = sin_ref[j, hd] * dec + u_h[j * DK:(j + 1) * DK]

    conv = _proj(h, wt_ref, R_CONV, N_IN)
    hc = conv[:, 2 * D_CONV:3 * D_CONV] * conv[:, 0:D_CONV]
    cprev = cprev_ref[...]
    t_in_seq = _iota(hc.shape, 0) & (DEC_LEN - 1)
    h1 = jnp.where(t_in_seq == 0, pltpu.roll(cprev, nt - 1, 0), pltpu.roll(hc, 1, 0))
    h2 = jnp.where(t_in_seq < 2, cprev, pltpu.roll(hc, 2, 0))
    cw = cw_ref[...]
    yc = cw[0:1] * h2 + cw[1:2] * h1 + cw[2:3] * hc
    yc = conv[:, D_CONV:2 * D_CONV] * yc * _silu(conv[:, 3 * D_CONV:4 * D_CONV])
    hc_ref[...] = hc

    y_ref[...] = _out_proj(x, o, yc, wo_ref, fg_ref)


def _const_spec(shape):
    return pl.BlockSpec(shape, lambda *_: (0,) * len(shape))


def _weight_specs():
    return [
        _const_spec((1, D_MODEL)),
        _const_spec((N_IN, D_MODEL)),
        _const_spec((RANK, D_QK)),
        _const_spec((1, D_QK)),
        _const_spec((1, DV)),
        _const_spec((3, D_CONV)),
        _const_spec((D_MODEL, D_MODEL)),
        _const_spec((1, D_MODEL)),
    ]


def kernel(x_prompt, x_sample, state_gla, state_conv, norm_gain, w_in, w_gk_up, b_gk,
           gla_norm_gain, conv_w, w_out, final_norm_gain):
    n_batch, seq_len, _ = x_prompt.shape
    n_dec = x_sample.shape[0]
    tiles_per_seq = seq_len // TILE
    n_tiles = n_batch * tiles_per_seq
    weights = (
        norm_gain.reshape(1, D_MODEL),
        jnp.swapaxes(w_in[0], 0, 1).astype(BF16),
        w_gk_up[0].astype(BF16),
        b_gk.reshape(1, D_QK),
        gla_norm_gain.reshape(1, DV),
        conv_w[0],
        w_out[0].astype(BF16),
        final_norm_gain.reshape(1, D_MODEL),
    )

    def in_tile(j):
        t = jnp.minimum(j, n_tiles - 1)
        return (t // tiles_per_seq, t % tiles_per_seq, 0)

    def out_tile(j):
        t = jnp.maximum(j - 1, 0)
        return (t // tiles_per_seq, t % tiles_per_seq, 0)

    def out_seq(j):
        return jnp.maximum(j - 1, 0) // tiles_per_seq

    y_p, s_p, c_p = pl.pallas_call(
        functools.partial(_prompt_kernel, tiles_per_seq),
        grid=(n_tiles + 1,),
        in_specs=[pl.BlockSpec((None, TILE, D_MODEL), in_tile)] + _weight_specs(),
        out_specs=[
            pl.BlockSpec((None, TILE, D_MODEL), out_tile),
            pl.BlockSpec((None, None, HEADS, DK, DV), lambda j: (0, out_seq(j), 0, 0, 0)),
            pl.BlockSpec((None, None, 2, D_CONV), lambda j: (0, out_seq(j), 0, 0)),
        ],
        out_shape=[
            jax.ShapeDtypeStruct((n_batch, seq_len, D_MODEL), F32),
            jax.ShapeDtypeStruct((1, n_batch, HEADS, DK, DV), F32),
            jax.ShapeDtypeStruct((1, n_batch, 2, D_CONV), F32),
        ],
        scratch_shapes=[
            pltpu.VMEM((2, TILE, C_END), F32),
            pltpu.VMEM((2, TILE, D_GLA), BF16),
            pltpu.VMEM((HEADS, DK, DV), F32),
            pltpu.VMEM((D_QK, D_GLA), BF16),
            pltpu.VMEM((8, D_CONV), F32),
        ],
        compiler_params=pltpu.CompilerParams(
            dimension_semantics=("arbitrary",), vmem_limit_bytes=VMEM_LIMIT),
        name="gla_conv_prompt",
    )(x_prompt, *weights)

    nt = SEQ_BLK * DEC_LEN
    xs = x_sample.reshape(n_dec * DEC_LEN, D_MODEL)
    cprev = jnp.pad(state_conv[0], ((0, 0), (0, DEC_LEN - 2), (0, 0))).reshape(n_dec * DEC_LEN, D_CONV)
    y_s, s_s, hc_s = pl.pallas_call(
        _sample_kernel,
        grid=(n_dec // SEQ_BLK,),
        in_specs=[
            pl.BlockSpec((nt, D_MODEL), lambda i: (i, 0)),
            pl.BlockSpec((nt, D_CONV), lambda i: (i, 0)),
            pl.BlockSpec((None, SEQ_BLK, HEADS, DK, DV), lambda i: (0, i, 0, 0, 0)),
        ] + _weight_specs(),
        out_specs=[
            pl.BlockSpec((nt, D_MODEL), lambda i: (i, 0)),
            pl.BlockSpec((None, SEQ_BLK, HEADS, DK, DV), lambda i: (0, i, 0, 0, 0)),
            pl.BlockSpec((nt, D_CONV), lambda i: (i, 0)),
        ],
        out_shape=[
            jax.ShapeDtypeStruct((n_dec * DEC_LEN, D_MODEL), F32),
            jax.ShapeDtypeStruct((1, n_dec, HEADS, DK, DV), F32),
            jax.ShapeDtypeStruct((n_dec * DEC_LEN, D_CONV), F32),
        ],
        compiler_params=pltpu.CompilerParams(
            dimension_semantics=("arbitrary",), vmem_limit_bytes=VMEM_LIMIT),
        name="gla_conv_sample",
    )(xs, cprev, state_gla, *weights)

    y_sample = y_s.reshape(n_dec, DEC_LEN, D_MODEL)
    conv_sample = hc_s.reshape(n_dec, DEC_LEN, D_CONV)[:, DEC_LEN - 2:][None]
    return (y_p, y_sample, s_p, c_p, s_s, conv_sample)
```

```python
import functools

import jax
import jax.numpy as jnp
from jax import lax
from jax.experimental import pallas as pl
from jax.experimental.pallas import tpu as pltpu

D_MODEL = 1024
HEADS = 4
DK = 64
DV = 128
D_QK = HEADS * DK
D_GLA = HEADS * DV
D_CONV = 512
RANK = 16
CHUNK = 64
TILE = 256
DEC_LEN = 4
SEQ_BLK = 32
GRP = 16
EPS = 1e-6
Q_SCALE = DK ** -0.5
GATE_SCALE = 1.0 / 16.0
VMEM_LIMIT = 52 * 1024 * 1024

R_QK = 0
R_V = 2 * D_QK
R_GATE = R_V + D_GLA
R_LR = R_GATE + D_GLA
R_CONV = R_LR + RANK
N_IN = R_CONV + 4 * D_CONV

C_X = 0
C_QK = C_X + D_MODEL
C_GATE = C_QK + 2 * D_QK
C_CONV = C_GATE + D_GLA
C_LR = C_CONV + 4 * D_CONV
C_END = C_LR + 128

F32 = jnp.float32
BF16 = jnp.bfloat16


def _dot(a, b):
    return jnp.dot(a, b, preferred_element_type=F32)


def _dot_nt(a, b):
    return lax.dot_general(a, b, (((1,), (1,)), ((), ())), preferred_element_type=F32)


def _proj(h, wt_ref, lo, hi):
    return _dot_nt(h, wt_ref[lo:hi])


def _dot_tn(a, b):
    return lax.dot_general(a, b, (((0,), (0,)), ((), ())), preferred_element_type=F32)


def _rmsnorm(x, gain):
    ms = jnp.mean(x * x, axis=-1, keepdims=True)
    return x * lax.rsqrt(ms + EPS) * gain


def _silu(x):
    return x * (1.0 / (1.0 + jnp.exp(-x)))


def _log_sigmoid(z):
    return jnp.minimum(z, 0.0) - jnp.log(1.0 + jnp.exp(-jnp.abs(z)))


def _iota(shape, dim):
    return lax.broadcasted_iota(jnp.int32, shape, dim)


def _masked_sum(mask_bf16, g):
    g1 = g.astype(BF16)
    r1 = g - g1.astype(F32)
    g2 = r1.astype(BF16)
    g3 = (r1 - g2.astype(F32)).astype(BF16)
    return _dot(mask_bf16, g1) + _dot(mask_bf16, g2) + _dot(mask_bf16, g3)


def _head_stack(q):
    lane_head = _iota(q.shape, 1) >> 6
    return jnp.concatenate([jnp.where(lane_head == h, q, 0.0) for h in range(HEADS)], axis=0)


def _gate_log_decay(lr, wup_ref, bgk_ref):
    z = _dot(lr.astype(BF16), wup_ref[...]) + bgk_ref[...]
    return _log_sigmoid(z) * GATE_SCALE


def _gla_epilogue(o, gate, gng_ref):
    outs = []
    for hd in range(HEADS):
        oh = o[:, hd * DV:(hd + 1) * DV]
        outs.append(_rmsnorm(oh, gng_ref[...]))
    return jnp.concatenate(outs, axis=1) * _silu(gate)


def _out_proj(x, o, yc, wo_ref, fg_ref):
    mix = jnp.concatenate([o, yc], axis=1).astype(BF16)
    out = x + _dot(mix, wo_ref[...])
    return _rmsnorm(out, fg_ref[...])


ITEM_SCHEDULE = (0, 5, 0, 0, 0, 0, 0, 0, 0, 0, 0, 6, 3, 1)


def _project_items(x_ref, ng_ref, wt_ref, p_ref, pv_ref, slot):
    cache = {}

    def norm():
        x = x_ref[...]
        cache["h"] = _rmsnorm(x, ng_ref[...]).astype(BF16)
        p_ref[slot, :, C_X:C_QK] = x

    def to_p(row, col, width=256):
        def item():
            p_ref[slot, :, col:col + width] = _proj(cache["h"], wt_ref, row, row + width)
        return item

    def to_pv(off):
        def item():
            pv_ref[slot, :, off:off + 256] = _proj(
                cache["h"], wt_ref, R_V + off, R_V + off + 256).astype(BF16)
        return item

    items = [norm, to_p(R_LR, C_LR, RANK)]
    items += [to_p(R_QK + o, C_QK + o) for o in (0, 256)]
    items += [to_pv(o) for o in (0, 256)]
    items += [to_p(R_CONV + o, C_CONV + o) for o in range(0, 4 * D_CONV, 256)]
    items += [to_p(R_GATE + o, C_GATE + o) for o in (0, 256)]
    return items


def _finish_tile(p_ref, pv_ref, slot, wup_ref, bgk_ref, gng_ref, cw_ref, wo_ref, fg_ref,
                 y_ref, sout_ref, cout_ref, s_ref, sbd_ref, tail_ref, emit):
    g = _gate_log_decay(p_ref[slot, :, C_LR:C_LR + RANK], wup_ref, bgk_ref)
    emit()

    rt = _iota((TILE, TILE), 0)
    ct = _iota((TILE, TILE), 1)
    cmask = jnp.where(((rt >> 6) == (ct >> 6)) & (ct <= rt), 1.0, 0.0).astype(BF16)
    b = _masked_sum(cmask, g)
    emit()

    ar = _iota((HEADS * CHUNK, CHUNK), 0) & (CHUNK - 1)
    ac = _iota((HEADS * CHUNK, CHUNK), 1)
    causal = ac <= ar

    o_chunks = []
    for c in range(TILE // CHUNK):
        r0 = c * CHUNK
        bc = b[r0:r0 + CHUNK]
        qc = p_ref[slot, r0:r0 + CHUNK, C_QK:C_QK + D_QK] * Q_SCALE
        kc = p_ref[slot, r0:r0 + CHUNK, C_QK + D_QK:C_GATE]
        vc = pv_ref[slot, r0:r0 + CHUNK, :]
        bmid = bc[CHUNK // 2:CHUNK // 2 + 1]
        blast = bc[CHUNK - 1:CHUNK]
        q_in = (qc * jnp.exp(bc)).astype(BF16)
        q_a = qc * jnp.exp(bc - bmid)
        k_a = (kc * jnp.exp(bmid - bc)).astype(BF16)
        k_d = (kc * jnp.exp(blast - bc)).astype(BF16)

        a_all = _dot_nt(_head_stack(q_a).astype(BF16), k_a)
        a_all = jnp.where(causal, a_all, 0.0).astype(BF16)
        inter = _dot(q_in, sbd_ref[...])
        intra = jnp.concatenate(
            [_dot(a_all[hd * CHUNK:(hd + 1) * CHUNK], vc[:, hd * DV:(hd + 1) * DV])
             for hd in range(HEADS)], axis=1)
        o_chunks.append(inter + intra)

        u_all = _dot_tn(k_d, vc)
        dec = jnp.transpose(jnp.broadcast_to(jnp.exp(blast), (DV, D_QK)))
        for hd in range(HEADS):
            rows = slice(hd * DK, (hd + 1) * DK)
            cols = slice(hd * DV, (hd + 1) * DV)
            s_new = s_ref[hd] * dec[rows] + u_all[rows, cols]
            s_ref[hd] = s_new
            sbd_ref[rows, cols] = s_new.astype(BF16)
        emit()

    o = jnp.concatenate(o_chunks, axis=0)
    mix = []
    for hd in range(HEADS):
        cols = slice(hd * DV, (hd + 1) * DV)
        gate = p_ref[slot, :, C_GATE + hd * DV:C_GATE + (hd + 1) * DV]
        mix.append((_rmsnorm(o[:, cols], gng_ref[...]) * _silu(gate)).astype(BF16))
        emit()

    cw = cw_ref[...]
    row = _iota((TILE, 256), 0)
    for half in range(D_CONV // 256):
        cols = slice(half * 256, (half + 1) * 256)

        def conv_in(k, cols=cols):
            return p_ref[slot, :, C_CONV + k * D_CONV + cols.start:C_CONV + k * D_CONV + cols.stop]

        hc = conv_in(2) * conv_in(0)
        tail = tail_ref[:, cols]
        h1 = jnp.where(row == 0, tail[7:8], pltpu.roll(hc, 1, 0))
        h2 = jnp.where(row == 0, tail[6:7], jnp.where(row == 1, tail[7:8], pltpu.roll(hc, 2, 0)))
        yc = cw[0:1, cols] * h2 + cw[1:2, cols] * h1 + cw[2:3, cols] * hc
        mix.append((conv_in(1) * yc * _silu(conv_in(3))).astype(BF16))
        tail_ref[:, cols] = hc[TILE - 8:TILE]
        cout_ref[:, cols] = hc[TILE - 2:TILE]
        emit()

    out = p_ref[slot, :, C_X:C_QK] + _dot(jnp.concatenate(mix, axis=1), wo_ref[...])
    emit()
    half_rows = TILE // 2
    y_ref[0:half_rows] = _rmsnorm(out[0:half_rows], fg_ref[...])
    emit()
    y_ref[half_rows:TILE] = _rmsnorm(out[half_rows:TILE], fg_ref[...])
    sout_ref[...] = s_ref[...]


def _prompt_kernel(tiles_per_seq, x_ref, ng_ref, wt_ref, wup_ref, bgk_ref, gng_ref, cw_ref,
                   wo_ref, fg_ref,
                   y_ref, sout_ref, cout_ref,
                   p_ref, pv_ref, s_ref, sbd_ref, tail_ref):
    step = pl.program_id(0)

    @pl.when(step == 0)
    def _():
        p_ref[1] = jnp.zeros(p_ref.shape[1:], F32)
        pv_ref[1] = jnp.zeros(pv_ref.shape[1:], BF16)

    @pl.when(jnp.logical_or(step == 0, (step - 1) % tiles_per_seq == 0))
    def _():
        s_ref[...] = jnp.zeros_like(s_ref)
        sbd_ref[...] = jnp.zeros_like(sbd_ref)
        tail_ref[...] = jnp.zeros_like(tail_ref)

    def body(write_slot, read_slot):
        items = _project_items(x_ref, ng_ref, wt_ref, p_ref, pv_ref, write_slot)
        counts = iter(ITEM_SCHEDULE)
        items[0]()
        pending = iter(items[1:])

        def emit():
            for _ in range(next(counts)):
                next(pending)()

        _finish_tile(p_ref, pv_ref, read_slot, wup_ref, bgk_ref, gng_ref, cw_ref, wo_ref, fg_ref,
                     y_ref, sout_ref, cout_ref, s_ref, sbd_ref, tail_ref, emit)
        assert next(counts, None) is None and next(pending, None) is None

    @pl.when(step % 2 == 0)
    def _():
        body(0, 1)

    @pl.when(step % 2 == 1)
    def _():
        body(1, 0)


def _sample_kernel(x_ref, cprev_ref, sin_ref, ng_ref, wt_ref, wup_ref, bgk_ref,
                   gng_ref, cw_ref, wo_ref, fg_ref,
                   y_ref, sout_ref, hc_ref):
    nt = SEQ_BLK * DEC_LEN
    gt = GRP * DEC_LEN
    x = x_ref[...]
    h = _rmsnorm(x, ng_ref[...]).astype(BF16)

    qk = _proj(h, wt_ref, R_QK, R_V)
    v = _proj(h, wt_ref, R_V, R_GATE).astype(BF16)
    gate = _proj(h, wt_ref, R_GATE, R_LR)
    g = _gate_log_decay(_proj(h, wt_ref, R_LR, R_CONV), wup_ref, bgk_ref)

    rt = _iota((nt, nt), 0)
    ct = _iota((nt, nt), 1)
    same = (rt >> 2) == (ct >> 2)
    cmask = jnp.where(same & (ct <= rt), 1.0, 0.0).astype(BF16)
    fmask = jnp.where(same, 1.0, 0.0).astype(BF16)
    b = _masked_sum(cmask, g)
    bl = _masked_sum(fmask, g)

    q = qk[:, 0:D_QK] * Q_SCALE
    k = qk[:, D_QK:2 * D_QK]
    q_in = (q * jnp.exp(b)).astype(BF16)
    k_a = (k * jnp.exp(-b)).astype(BF16)
    k_d = k * jnp.exp(bl - b)
    k_dt = jnp.transpose(k_d).astype(BF16)
    dec_t = jnp.transpose(jnp.exp(bl))

    ar = _iota((HEADS * gt, gt), 0) & (gt - 1)
    ac = _iota((HEADS * gt, gt), 1)
    amask = ((ar >> 2) == (ac >> 2)) & (ac <= ar)
    ir = _iota((HEADS * gt, GRP * D_QK), 0)
    ic = _iota((HEADS * gt, GRP * D_QK), 1)
    imask = (((ir & (gt - 1)) >> 2) == (ic >> 8)) & ((ir >> 6) == ((ic >> 6) & 3))

    o_groups = []
    for gi in range(SEQ_BLK // GRP):
        r0 = gi * gt
        qg = q_in[r0:r0 + gt]
        a_all = _dot_nt(_head_stack(qg), k_a[r0:r0 + gt])
        a_all = jnp.where(amask, a_all, 0.0).astype(BF16)
        s_cat = sin_ref[gi * GRP:(gi + 1) * GRP].reshape(GRP * D_QK, DV).astype(BF16)
        lhs = jnp.where(imask, jnp.tile(qg, (HEADS, GRP)), 0.0)
        inter = _dot(lhs, s_cat)
        outs = []
        for hd in range(HEADS):
            rows = slice(hd * gt, (hd + 1) * gt)
            intra = _dot(a_all[rows], v[r0:r0 + gt, hd * DV:(hd + 1) * DV])
            outs.append(inter[rows] + intra)
        o_groups.append(jnp.concatenate(outs, axis=1))
    o = _gla_epilogue(jnp.concatenate(o_groups, axis=0), gate, gng_ref)

    ur = _iota((SEQ_BLK * DK, nt), 0)
    uc = _iota((SEQ_BLK * DK, nt), 1)
    umask = (ur >> 6) == (uc >> 2)
    for hd in range(HEADS):
        kt = jnp.tile(k_dt[hd * DK:(hd + 1) * DK], (SEQ_BLK, 1))
        u_h = _dot(jnp.where(umask, kt, 0.0), v[:, hd * DV:(hd + 1) * DV])
        for j in range(SEQ_BLK):
            col = DEC_LEN * j
            dec = jnp.broadcast_to(dec_t[hd * DK:(hd + 1) * DK, col:col + 1], (DK, DV))
            sout_ref[j, hd] = sin_ref[j, hd] * dec + u_h[j * DK:(j + 1) * DK]

    conv = _proj(h, wt_ref, R_CONV, N_IN)
    hc = conv[:, 2 * D_CONV:3 * D_CONV] * conv[:, 0:D_CONV]
    cprev = cprev_ref[...]
    t_in_seq = _iota(hc.shape, 0) & (DEC_LEN - 1)
    h1 = jnp.where(t_in_seq == 0, pltpu.roll(cprev, nt - 1, 0), pltpu.roll(hc, 1, 0))
    h2 = jnp.where(t_in_seq < 2, cprev, pltpu.roll(hc, 2, 0))
    cw = cw_ref[...]
    yc = cw[0:1] * h2 + cw[1:2] * h1 + cw[2:3] * hc
    yc = conv[:, D_CONV:2 * D_CONV] * yc * _silu(conv[:, 3 * D_CONV:4 * D_CONV])
    hc_ref[...] = hc

    y_ref[...] = _out_proj(x, o, yc, wo_ref, fg_ref)


def _const_spec(shape):
    return pl.BlockSpec(shape, lambda *_: (0,) * len(shape))


def _weight_specs():
    return [
        _const_spec((1, D_MODEL)),
        _const_spec((N_IN, D_MODEL)),
        _const_spec((RANK, D_QK)),
        _const_spec((1, D_QK)),
        _const_spec((1, DV)),
        _const_spec((3, D_CONV)),
        _const_spec((D_MODEL, D_MODEL)),
        _const_spec((1, D_MODEL)),
    ]


def kernel(x_prompt, x_sample, state_gla, state_conv, norm_gain, w_in, w_gk_up, b_gk,
           gla_norm_gain, conv_w, w_out, final_norm_gain):
    n_batch, seq_len, _ = x_prompt.shape
    n_dec = x_sample.shape[0]
    tiles_per_seq = seq_len // TILE
    n_tiles = n_batch * tiles_per_seq
    weights = (
        norm_gain.reshape(1, D_MODEL),
        jnp.swapaxes(w_in[0], 0, 1).astype(BF16),
        w_gk_up[0].astype(BF16),
        b_gk.reshape(1, D_QK),
        gla_norm_gain.reshape(1, DV),
        conv_w[0],
        w_out[0].astype(BF16),
        final_norm_gain.reshape(1, D_MODEL),
    )

    def in_tile(j):
        t = jnp.minimum(j, n_tiles - 1)
        return (t // tiles_per_seq, t % tiles_per_seq, 0)

    def out_tile(j):
        t = jnp.maximum(j - 1, 0)
        return (t // tiles_per_seq, t % tiles_per_seq, 0)

    def out_seq(j):
        return jnp.maximum(j - 1, 0) // tiles_per_seq

    y_p, s_p, c_p = pl.pallas_call(
        functools.partial(_prompt_kernel, tiles_per_seq),
        grid=(n_tiles + 1,),
        in_specs=[pl.BlockSpec((None, TILE, D_MODEL), in_tile)] + _weight_specs(),
        out_specs=[
            pl.BlockSpec((None, TILE, D_MODEL), out_tile),
            pl.BlockSpec((None, None, HEADS, DK, DV), lambda j: (0, out_seq(j), 0, 0, 0)),
            pl.BlockSpec((None, None, 2, D_CONV), lambda j: (0, out_seq(j), 0, 0)),
        ],
        out_shape=[
            jax.ShapeDtypeStruct((n_batch, seq_len, D_MODEL), F32),
            jax.ShapeDtypeStruct((1, n_batch, HEADS, DK, DV), F32),
            jax.ShapeDtypeStruct((1, n_batch, 2, D_CONV), F32),
        ],
        scratch_shapes=[
            pltpu.VMEM((2, TILE, C_END), F32),
            pltpu.VMEM((2, TILE, D_GLA), BF16),
            pltpu.VMEM((HEADS, DK, DV), F32),
            pltpu.VMEM((D_QK, D_GLA), BF16),
            pltpu.VMEM((8, D_CONV), F32),
        ],
        compiler_params=pltpu.CompilerParams(
            dimension_semantics=("arbitrary",), vmem_limit_bytes=VMEM_LIMIT),
        name="gla_conv_prompt",
    )(x_prompt, *weights)

    nt = SEQ_BLK * DEC_LEN
    xs = x_sample.reshape(n_dec * DEC_LEN, D_MODEL)
    cprev = jnp.pad(state_conv[0], ((0, 0), (0, DEC_LEN - 2), (0, 0))).reshape(n_dec * DEC_LEN, D_CONV)
    y_s, s_s, hc_s = pl.pallas_call(
        _sample_kernel,
        grid=(n_dec // SEQ_BLK,),
        in_specs=[
            pl.BlockSpec((nt, D_MODEL), lambda i: (i, 0)),
            pl.BlockSpec((nt, D_CONV), lambda i: (i, 0)),
            pl.BlockSpec((None, SEQ_BLK, HEADS, DK, DV), lambda i: (0, i, 0, 0, 0)),
        ] + _weight_specs(),
        out_specs=[
            pl.BlockSpec((nt, D_MODEL), lambda i: (i, 0)),
            pl.BlockSpec((None, SEQ_BLK, HEADS, DK, DV), lambda i: (0, i, 0, 0, 0)),
            pl.BlockSpec((nt, D_CONV), lambda i: (i, 0)),
        ],
        out_shape=[
            jax.ShapeDtypeStruct((n_dec * DEC_LEN, D_MODEL), F32),
            jax.ShapeDtypeStruct((1, n_dec, HEADS, DK, DV), F32),
            jax.ShapeDtypeStruct((n_dec * DEC_LEN, D_CONV), F32),
        ],
        compiler_params=pltpu.CompilerParams(
            dimension_semantics=("arbitrary",), vmem_limit_bytes=VMEM_LIMIT),
        name="gla_conv_sample",
    )(xs, cprev, state_gla, *weights)

    y_sample = y_s.reshape(n_dec, DEC_LEN, D_MODEL)
    conv_sample = hc_s.reshape(n_dec, DEC_LEN, D_CONV)[:, DEC_LEN - 2:][None]
    return (y_p, y_sample, s_p, c_p, s_s, conv_sample)
```

```python
import functools

import jax
import jax.numpy as jnp
from jax import lax
from jax.experimental import pallas as pl
from jax.experimental.pallas import tpu as pltpu

D_MODEL = 1024
HEADS = 4
DK = 64
DV = 128
D_QK = HEADS * DK
D_GLA = HEADS * DV
D_CONV = 512
RANK = 16
CHUNK = 64
TILE = 256
DEC_LEN = 4
SEQ_BLK = 32
GRP = 16
EPS = 1e-6
Q_SCALE = DK ** -0.5
GATE_SCALE = 1.0 / 16.0
VMEM_LIMIT = 52 * 1024 * 1024

R_QK = 0
R_V = 2 * D_QK
R_GATE = R_V + D_GLA
R_CONV = R_GATE + D_GLA
R_LR = R_CONV + 4 * D_CONV
PREP_BLK = 512
N_W = R_LR + PREP_BLK
SRC_LR = 2 * D_QK + 2 * D_GLA
SRC_CONV = SRC_LR + RANK

C_X = 0
C_QK = C_X + D_MODEL
C_GATE = C_QK + 2 * D_QK
C_CONV = C_GATE + D_GLA
C_LR = C_CONV + 4 * D_CONV
C_END = C_LR + 128

F32 = jnp.float32
BF16 = jnp.bfloat16


def _dot(a, b):
    return jnp.dot(a, b, preferred_element_type=F32)


def _dot_nt(a, b):
    return lax.dot_general(a, b, (((1,), (1,)), ((), ())), preferred_element_type=F32)


def _proj(h, w_ref, lo, hi):
    return _dot(h, w_ref[:, lo:hi])


def _dot_tn(a, b):
    return lax.dot_general(a, b, (((0,), (0,)), ((), ())), preferred_element_type=F32)


def _rmsnorm(x, gain):
    ms = jnp.mean(x * x, axis=-1, keepdims=True)
    return x * lax.rsqrt(ms + EPS) * gain


def _silu(x):
    return x * (1.0 / (1.0 + jnp.exp(-x)))


def _log_sigmoid(z):
    return jnp.minimum(z, 0.0) - jnp.log(1.0 + jnp.exp(-jnp.abs(z)))


def _iota(shape, dim):
    return lax.broadcasted_iota(jnp.int32, shape, dim)


def _masked_sum(mask_bf16, g):
    g1 = g.astype(BF16)
    r1 = g - g1.astype(F32)
    g2 = r1.astype(BF16)
    g3 = (r1 - g2.astype(F32)).astype(BF16)
    return _dot(mask_bf16, g1) + _dot(mask_bf16, g2) + _dot(mask_bf16, g3)


def _head_stack(q):
    lane_head = _iota(q.shape, 1) >> 6
    return jnp.concatenate([jnp.where(lane_head == h, q, 0.0) for h in range(HEADS)], axis=0)


def _gate_log_decay(lr, wup_ref, bgk_ref):
    z = _dot(lr.astype(BF16), wup_ref[...]) + bgk_ref[...]
    return _log_sigmoid(z) * GATE_SCALE


def _gla_epilogue(o, gate, gng_ref):
    outs = []
    for hd in range(HEADS):
        oh = o[:, hd * DV:(hd + 1) * DV]
        outs.append(_rmsnorm(oh, gng_ref[...]))
    return jnp.concatenate(outs, axis=1) * _silu(gate)


def _out_proj(x, o, yc, wo_ref, fg_ref):
    mix = jnp.concatenate([o, yc], axis=1).astype(BF16)
    out = x + _dot(mix, wo_ref[...])
    return _rmsnorm(out, fg_ref[...])


ITEM_SCHEDULE = (0, 5, 0, 0, 0, 0, 0, 0, 0, 0, 0, 6, 3, 1)


def _project_items(x_ref, ng_ref, wt_ref, p_ref, pv_ref, slot):
    cache = {}

    def norm():
        x = x_ref[...]
        cache["h"] = _rmsnorm(x, ng_ref[...]).astype(BF16)
        p_ref[slot, :, C_X:C_QK] = x

    def to_p(row, col, width=256):
        def item():
            p_ref[slot, :, col:col + width] = _proj(cache["h"], wt_ref, row, row + width)
        return item

    def to_pv(off):
        def item():
            pv_ref[slot, :, off:off + 256] = _proj(
                cache["h"], wt_ref, R_V + off, R_V + off + 256).astype(BF16)
        return item

    items = [norm, to_p(R_LR, C_LR, RANK)]
    items += [to_p(R_QK + o, C_QK + o) for o in (0, 256)]
    items += [to_pv(o) for o in (0, 256)]
    items += [to_p(R_CONV + o, C_CONV + o) for o in range(0, 4 * D_CONV, 256)]
    items += [to_p(R_GATE + o, C_GATE + o) for o in (0, 256)]
    return items


def _finish_tile(p_ref, pv_ref, slot, wup_ref, bgk_ref, gng_ref, cw_ref, wo_ref, fg_ref,
                 y_ref, sout_ref, cout_ref, s_ref, sbd_ref, tail_ref, emit):
    g = _gate_log_decay(p_ref[slot, :, C_LR:C_LR + RANK], wup_ref, bgk_ref)
    emit()

    rt = _iota((TILE, TILE), 0)
    ct = _iota((TILE, TILE), 1)
    cmask = jnp.where(((rt >> 6) == (ct >> 6)) & (ct <= rt), 1.0, 0.0).astype(BF16)
    b = _masked_sum(cmask, g)
    emit()

    ar = _iota((HEADS * CHUNK, CHUNK), 0) & (CHUNK - 1)
    ac = _iota((HEADS * CHUNK, CHUNK), 1)
    causal = ac <= ar

    o_chunks = []
    for c in range(TILE // CHUNK):
        r0 = c * CHUNK
        bc = b[r0:r0 + CHUNK]
        qc = p_ref[slot, r0:r0 + CHUNK, C_QK:C_QK + D_QK] * Q_SCALE
        kc = p_ref[slot, r0:r0 + CHUNK, C_QK + D_QK:C_GATE]
        vc = pv_ref[slot, r0:r0 + CHUNK, :]
        bmid = bc[CHUNK // 2:CHUNK // 2 + 1]
        blast = bc[CHUNK - 1:CHUNK]
        q_in = (qc * jnp.exp(bc)).astype(BF16)
        q_a = qc * jnp.exp(bc - bmid)
        k_a = (kc * jnp.exp(bmid - bc)).astype(BF16)
        k_d = (kc * jnp.exp(blast - bc)).astype(BF16)

        a_all = _dot_nt(_head_stack(q_a).astype(BF16), k_a)
        a_all = jnp.where(causal, a_all, 0.0).astype(BF16)
        inter = _dot(q_in, sbd_ref[...])
        intra = jnp.concatenate(
            [_dot(a_all[hd * CHUNK:(hd + 1) * CHUNK], vc[:, hd * DV:(hd + 1) * DV])
             for hd in range(HEADS)], axis=1)
        o_chunks.append(inter + intra)

        u_all = _dot_tn(k_d, vc)
        dec = jnp.transpose(jnp.broadcast_to(jnp.exp(blast), (DV, D_QK)))
        for hd in range(HEADS):
            rows = slice(hd * DK, (hd + 1) * DK)
            cols = slice(hd * DV, (hd + 1) * DV)
            s_new = s_ref[hd] * dec[rows] + u_all[rows, cols]
            s_ref[hd] = s_new
            sbd_ref[rows, cols] = s_new.astype(BF16)
        emit()

    o = jnp.concatenate(o_chunks, axis=0)
    mix = []
    for hd in range(HEADS):
        cols = slice(hd * DV, (hd + 1) * DV)
        gate = p_ref[slot, :, C_GATE + hd * DV:C_GATE + (hd + 1) * DV]
        mix.append((_rmsnorm(o[:, cols], gng_ref[...]) * _silu(gate)).astype(BF16))
        emit()

    cw = cw_ref[...]
    row = _iota((TILE, 256), 0)
    for half in range(D_CONV // 256):
        cols = slice(half * 256, (half + 1) * 256)

        def conv_in(k, cols=cols):
            return p_ref[slot, :, C_CONV + k * D_CONV + cols.start:C_CONV + k * D_CONV + cols.stop]

        hc = conv_in(2) * conv_in(0)
        tail = tail_ref[:, cols]
        h1 = jnp.where(row == 0, tail[7:8], pltpu.roll(hc, 1, 0))
        h2 = jnp.where(row == 0, tail[6:7], jnp.where(row == 1, tail[7:8], pltpu.roll(hc, 2, 0)))
        yc = cw[0:1, cols] * h2 + cw[1:2, cols] * h1 + cw[2:3, cols] * hc
        mix.append((conv_in(1) * yc * _silu(conv_in(3))).astype(BF16))
        tail_ref[:, cols] = hc[TILE - 8:TILE]
        cout_ref[:, cols] = hc[TILE - 2:TILE]
        emit()

    out = p_ref[slot, :, C_X:C_QK] + _dot(jnp.concatenate(mix, axis=1), wo_ref[...])
    emit()
    half_rows = TILE // 2
    y_ref[0:half_rows] = _rmsnorm(out[0:half_rows], fg_ref[...])
    emit()
    y_ref[half_rows:TILE] = _rmsnorm(out[half_rows:TILE], fg_ref[...])
    sout_ref[...] = s_ref[...]


def _prompt_kernel(tiles_per_seq, x_ref, ng_ref, wt_ref, wup_ref, bgk_ref, gng_ref, cw_ref,
                   wo_ref, fg_ref,
                   y_ref, sout_ref, cout_ref,
                   p_ref, pv_ref, s_ref, sbd_ref, tail_ref):
    step = pl.program_id(0)

    @pl.when(step == 0)
    def _():
        p_ref[1] = jnp.zeros(p_ref.shape[1:], F32)
        pv_ref[1] = jnp.zeros(pv_ref.shape[1:], BF16)

    @pl.when(jnp.logical_or(step == 0, (step - 1) % tiles_per_seq == 0))
    def _():
        s_ref[...] = jnp.zeros_like(s_ref)
        sbd_ref[...] = jnp.zeros_like(sbd_ref)
        tail_ref[...] = jnp.zeros_like(tail_ref)

    def body(write_slot, read_slot):
        items = _project_items(x_ref, ng_ref, wt_ref, p_ref, pv_ref, write_slot)
        counts = iter(ITEM_SCHEDULE)
        items[0]()
        pending = iter(items[1:])

        def emit():
            for _ in range(next(counts)):
                next(pending)()

        _finish_tile(p_ref, pv_ref, read_slot, wup_ref, bgk_ref, gng_ref, cw_ref, wo_ref, fg_ref,
                     y_ref, sout_ref, cout_ref, s_ref, sbd_ref, tail_ref, emit)
        assert next(counts, None) is None and next(pending, None) is None

    @pl.when(step % 2 == 0)
    def _():
        body(0, 1)

    @pl.when(step % 2 == 1)
    def _():
        body(1, 0)


def _sample_kernel(x_ref, cprev_ref, sin_ref, ng_ref, wt_ref, wup_ref, bgk_ref,
                   gng_ref, cw_ref, wo_ref, fg_ref,
                   y_ref, sout_ref, hc_ref):
    nt = SEQ_BLK * DEC_LEN
    gt = GRP * DEC_LEN
    x = x_ref[...]
    h = _rmsnorm(x, ng_ref[...]).astype(BF16)

    qk = _proj(h, wt_ref, R_QK, R_V)
    v = _proj(h, wt_ref, R_V, R_GATE).astype(BF16)
    gate = _proj(h, wt_ref, R_GATE, R_CONV)
    g = _gate_log_decay(_proj(h, wt_ref, R_LR, R_LR + RANK), wup_ref, bgk_ref)

    rt = _iota((nt, nt), 0)
    ct = _iota((nt, nt), 1)
    same = (rt >> 2) == (ct >> 2)
    cmask = jnp.where(same & (ct <= rt), 1.0, 0.0).astype(BF16)
    fmask = jnp.where(same, 1.0, 0.0).astype(BF16)
    b = _masked_sum(cmask, g)
    bl = _masked_sum(fmask, g)

    q = qk[:, 0:D_QK] * Q_SCALE
    k = qk[:, D_QK:2 * D_QK]
    q_in = (q * jnp.exp(b)).astype(BF16)
    k_a = (k * jnp.exp(-b)).astype(BF16)
    k_d = k * jnp.exp(bl - b)
    k_dt = jnp.transpose(k_d).astype(BF16)
    dec_t = jnp.transpose(jnp.exp(bl))

    ar = _iota((HEADS * gt, gt), 0) & (gt - 1)
    ac = _iota((HEADS * gt, gt), 1)
    amask = ((ar >> 2) == (ac >> 2)) & (ac <= ar)
    ir = _iota((HEADS * gt, GRP * D_QK), 0)
    ic = _iota((HEADS * gt, GRP * D_QK), 1)
    imask = (((ir & (gt - 1)) >> 2) == (ic >> 8)) & ((ir >> 6) == ((ic >> 6) & 3))

    o_groups = []
    for gi in range(SEQ_BLK // GRP):
        r0 = gi * gt
        qg = q_in[r0:r0 + gt]
        a_all = _dot_nt(_head_stack(qg), k_a[r0:r0 + gt])
        a_all = jnp.where(amask, a_all, 0.0).astype(BF16)
        s_cat = sin_ref[gi * GRP:(gi + 1) * GRP].reshape(GRP * D_QK, DV).astype(BF16)
        lhs = jnp.where(imask, jnp.tile(qg, (HEADS, GRP)), 0.0)
        inter = _dot(lhs, s_cat)
        outs = []
        for hd in range(HEADS):
            rows = slice(hd * gt, (hd + 1) * gt)
            intra = _dot(a_all[rows], v[r0:r0 + gt, hd * DV:(hd + 1) * DV])
            outs.append(inter[rows] + intra)
        o_groups.append(jnp.concatenate(outs, axis=1))
    o = _gla_epilogue(jnp.concatenate(o_groups, axis=0), gate, gng_ref)

    ur = _iota((SEQ_BLK * DK, nt), 0)
    uc = _iota((SEQ_BLK * DK, nt), 1)
    umask = (ur >> 6) == (uc >> 2)
    for hd in range(HEADS):
        kt = jnp.tile(k_dt[hd * DK:(hd + 1) * DK], (SEQ_BLK, 1))
        u_h = _dot(jnp.where(umask, kt, 0.0), v[:, hd * DV:(hd + 1) * DV])
        for j in range(SEQ_BLK):
            col = DEC_LEN * j
            dec = jnp.broadcast_to(dec_t[hd * DK:(hd + 1) * DK, col:col + 1], (DK, DV))
            sout_ref[j, hd] = sin_ref[j, hd] * dec + u_h[j * DK:(j + 1) * DK]

    conv = _proj(h, wt_ref, R_CONV, R_LR)
    hc = conv[:, 2 * D_CONV:3 * D_CONV] * conv[:, 0:D_CONV]
    cprev = cprev_ref[...]
    t_in_seq = _iota(hc.shape, 0) & (DEC_LEN - 1)
    h1 = jnp.where(t_in_seq == 0, pltpu.roll(cprev, nt - 1, 0), pltpu.roll(hc, 1, 0))
    h2 = jnp.where(t_in_seq < 2, cprev, pltpu.roll(hc, 2, 0))
    cw = cw_ref[...]
    yc = cw[0:1] * h2 + cw[1:2] * h1 + cw[2:3] * hc
    yc = conv[:, D_CONV:2 * D_CONV] * yc * _silu(conv[:, 3 * D_CONV:4 * D_CONV])
    hc_ref[...] = hc

    y_ref[...] = _out_proj(x, o, yc, wo_ref, fg_ref)


def _const_spec(shape):
    return pl.BlockSpec(shape, lambda *_: (0,) * len(shape))


def _weight_specs():
    return [
        _const_spec((1, D_MODEL)),
        _const_spec((D_MODEL, N_W)),
        _const_spec((RANK, D_QK)),
        _const_spec((1, D_QK)),
        _const_spec((1, DV)),
        _const_spec((3, D_CONV)),
        _const_spec((D_MODEL, D_MODEL)),
        _const_spec((1, D_MODEL)),
    ]


def _prep_kernel(wt_ref, wo_ref, w_ref, wob_ref):
    w_ref[...] = jnp.transpose(wt_ref[...]).astype(BF16)
    wob_ref[...] = wo_ref[...].astype(BF16)


def _prepare_weights(w_in, w_out):
    n_qkvg = SRC_LR // PREP_BLK
    n_conv = (4 * D_CONV) // PREP_BLK
    n_blk = N_W // PREP_BLK
    assert n_blk == n_qkvg + n_conv + 1

    def src_row(i):
        row = jnp.where(i < n_qkvg, PREP_BLK * i,
                        jnp.where(i < n_qkvg + n_conv, SRC_CONV + PREP_BLK * (i - n_qkvg), SRC_LR))
        return pl.multiple_of(row, RANK)

    wo_rows = D_MODEL // n_blk
    return pl.pallas_call(
        _prep_kernel,
        grid=(n_blk,),
        in_specs=[
            pl.BlockSpec((pl.Element(PREP_BLK), pl.Element(D_MODEL)), lambda i: (src_row(i), 0)),
            pl.BlockSpec((wo_rows, D_MODEL), lambda i: (i, 0)),
        ],
        out_specs=[
            pl.BlockSpec((D_MODEL, PREP_BLK), lambda i: (0, i)),
            pl.BlockSpec((wo_rows, D_MODEL), lambda i: (i, 0)),
        ],
        out_shape=[
            jax.ShapeDtypeStruct((D_MODEL, N_W), BF16),
            jax.ShapeDtypeStruct((D_MODEL, D_MODEL), BF16),
        ],
        compiler_params=pltpu.CompilerParams(dimension_semantics=("arbitrary",)),
        name="weight_prep",
    )(jnp.swapaxes(w_in[0], 0, 1), w_out[0])


def kernel(x_prompt, x_sample, state_gla, state_conv, norm_gain, w_in, w_gk_up, b_gk,
           gla_norm_gain, conv_w, w_out, final_norm_gain):
    n_batch, seq_len, _ = x_prompt.shape
    n_dec = x_sample.shape[0]
    tiles_per_seq = seq_len // TILE
    n_tiles = n_batch * tiles_per_seq
    w_proj, w_o = _prepare_weights(w_in, w_out)
    weights = (
        norm_gain.reshape(1, D_MODEL),
        w_proj,
        w_gk_up[0].astype(BF16),
        b_gk.reshape(1, D_QK),
        gla_norm_gain.reshape(1, DV),
        conv_w[0],
        w_o,
        final_norm_gain.reshape(1, D_MODEL),
    )

    def in_tile(j):
        t = jnp.minimum(j, n_tiles - 1)
        return (t // tiles_per_seq, t % tiles_per_seq, 0)

    def out_tile(j):
        t = jnp.maximum(j - 1, 0)
        return (t // tiles_per_seq, t % tiles_per_seq, 0)

    def out_seq(j):
        return jnp.maximum(j - 1, 0) // tiles_per_seq

    y_p, s_p, c_p = pl.pallas_call(
        functools.partial(_prompt_kernel, tiles_per_seq),
        grid=(n_tiles + 1,),
        in_specs=[pl.BlockSpec((None, TILE, D_MODEL), in_tile)] + _weight_specs(),
        out_specs=[
            pl.BlockSpec((None, TILE, D_MODEL), out_tile),
            pl.BlockSpec((None, None, HEADS, DK, DV), lambda j: (0, out_seq(j), 0, 0, 0)),
            pl.BlockSpec((None, None, 2, D_CONV), lambda j: (0, out_seq(j), 0, 0)),
        ],
        out_shape=[
            jax.ShapeDtypeStruct((n_batch, seq_len, D_MODEL), F32),
            jax.ShapeDtypeStruct((1, n_batch, HEADS, DK, DV), F32),
            jax.ShapeDtypeStruct((1, n_batch, 2, D_CONV), F32),
        ],
        scratch_shapes=[
            pltpu.VMEM((2, TILE, C_END), F32),
            pltpu.VMEM((2, TILE, D_GLA), BF16),
            pltpu.VMEM((HEADS, DK, DV), F32),
            pltpu.VMEM((D_QK, D_GLA), BF16),
            pltpu.VMEM((8, D_CONV), F32),
        ],
        compiler_params=pltpu.CompilerParams(
            dimension_semantics=("arbitrary",), vmem_limit_bytes=VMEM_LIMIT),
        name="gla_conv_prompt",
    )(x_prompt, *weights)

    nt = SEQ_BLK * DEC_LEN
    xs = x_sample.reshape(n_dec * DEC_LEN, D_MODEL)
    cprev = jnp.pad(state_conv[0], ((0, 0), (0, DEC_LEN - 2), (0, 0))).reshape(n_dec * DEC_LEN, D_CONV)
    y_s, s_s, hc_s = pl.pallas_call(
        _sample_kernel,
        grid=(n_dec // SEQ_BLK,),
        in_specs=[
            pl.BlockSpec((nt, D_MODEL), lambda i: (i, 0)),
            pl.BlockSpec((nt, D_CONV), lambda i: (i, 0)),
            pl.BlockSpec((None, SEQ_BLK, HEADS, DK, DV), lambda i: (0, i, 0, 0, 0)),
        ] + _weight_specs(),
        out_specs=[
            pl.BlockSpec((nt, D_MODEL), lambda i: (i, 0)),
            pl.BlockSpec((None, SEQ_BLK, HEADS, DK, DV), lambda i: (0, i, 0, 0, 0)),
            pl.BlockSpec((nt, D_CONV), lambda i: (i, 0)),
        ],
        out_shape=[
            jax.ShapeDtypeStruct((n_dec * DEC_LEN, D_MODEL), F32),
            jax.ShapeDtypeStruct((1, n_dec, HEADS, DK, DV), F32),
            jax.ShapeDtypeStruct((n_dec * DEC_LEN, D_CONV), F32),
        ],
        compiler_params=pltpu.CompilerParams(
            dimension_semantics=("arbitrary",), vmem_limit_bytes=VMEM_LIMIT),
        name="gla_conv_sample",
    )(xs, cprev, state_gla, *weights)

    y_sample = y_s.reshape(n_dec, DEC_LEN, D_MODEL)
    conv_sample = hc_s.reshape(n_dec, DEC_LEN, D_CONV)[:, DEC_LEN - 2:][None]
    return (y_p, y_sample, s_p, c_p, s_s, conv_sample)
```

```python
import functools

import jax
import jax.numpy as jnp
from jax import lax
from jax.experimental import pallas as pl
from jax.experimental.pallas import tpu as pltpu

D_MODEL = 1024
HEADS = 4
DK = 64
DV = 128
D_QK = HEADS * DK
D_GLA = HEADS * DV
D_CONV = 512
RANK = 16
CHUNK = 64
TILE = 256
DEC_LEN = 4
SEQ_BLK = 32
GRP = 16
EPS = 1e-6
Q_SCALE = DK ** -0.5
GATE_SCALE = 1.0 / 16.0
VMEM_LIMIT = 52 * 1024 * 1024

R_QK = 0
R_V = 2 * D_QK
R_GATE = R_V + D_GLA
R_CONV = R_GATE + D_GLA
R_LR = R_CONV + 4 * D_CONV
PREP_BLK = 512
N_W = R_LR + PREP_BLK
SRC_LR = 2 * D_QK + 2 * D_GLA
SRC_CONV = SRC_LR + RANK

C_X = 0
C_QK = C_X + D_MODEL
C_GATE = C_QK + 2 * D_QK
C_CONV = C_GATE + D_GLA
C_LR = C_CONV + 4 * D_CONV
C_END = C_LR + 128

F32 = jnp.float32
BF16 = jnp.bfloat16


def _dot(a, b):
    return jnp.dot(a, b, preferred_element_type=F32)


def _dot_nt(a, b):
    return lax.dot_general(a, b, (((1,), (1,)), ((), ())), preferred_element_type=F32)


def _proj(h, w_ref, lo, hi):
    return _dot(h, w_ref[:, lo:hi])


def _dot_tn(a, b):
    return lax.dot_general(a, b, (((0,), (0,)), ((), ())), preferred_element_type=F32)


def _rmsnorm(x, gain):
    ms = jnp.mean(x * x, axis=-1, keepdims=True)
    return x * lax.rsqrt(ms + EPS) * gain


def _silu(x):
    return x * (1.0 / (1.0 + jnp.exp(-x)))


def _log_sigmoid(z):
    return jnp.minimum(z, 0.0) - jnp.log(1.0 + jnp.exp(-jnp.abs(z)))


def _iota(shape, dim):
    return lax.broadcasted_iota(jnp.int32, shape, dim)


def _masked_sum(mask_bf16, g):
    g1 = g.astype(BF16)
    r1 = g - g1.astype(F32)
    g2 = r1.astype(BF16)
    g3 = (r1 - g2.astype(F32)).astype(BF16)
    return _dot(mask_bf16, g1) + _dot(mask_bf16, g2) + _dot(mask_bf16, g3)


def _head_stack(q):
    lane_head = _iota(q.shape, 1) >> 6
    return jnp.concatenate([jnp.where(lane_head == h, q, 0.0) for h in range(HEADS)], axis=0)


def _gate_log_decay(lr, wup_ref, bgk_ref):
    z = _dot(lr.astype(BF16), wup_ref[...]) + bgk_ref[...]
    return _log_sigmoid(z) * GATE_SCALE


def _gla_epilogue(o, gate, gng_ref):
    outs = []
    for hd in range(HEADS):
        oh = o[:, hd * DV:(hd + 1) * DV]
        outs.append(_rmsnorm(oh, gng_ref[...]))
    return jnp.concatenate(outs, axis=1) * _silu(gate)


def _out_proj(x, o, yc, wo_ref, fg_ref):
    mix = jnp.concatenate([o, yc], axis=1).astype(BF16)
    out = x + _dot(mix, wo_ref[...])
    return _rmsnorm(out, fg_ref[...])


ITEM_SCHEDULE = (0, 5, 0, 0, 0, 0, 0, 0, 0, 0, 0, 6, 3, 1)


def _project_items(x_ref, ng_ref, wt_ref, p_ref, pv_ref, slot):
    cache = {}

    def norm():
        x = x_ref[...]
        cache["h"] = _rmsnorm(x, ng_ref[...]).astype(BF16)
        p_ref[slot, :, C_X:C_QK] = x

    def to_p(row, col, width=256):
        def item():
            p_ref[slot, :, col:col + width] = _proj(cache["h"], wt_ref, row, row + width)
        return item

    def to_pv(off):
        def item():
            pv_ref[slot, :, off:off + 256] = _proj(
                cache["h"], wt_ref, R_V + off, R_V + off + 256).astype(BF16)
        return item

    items = [norm, to_p(R_LR, C_LR, RANK)]
    items += [to_p(R_QK + o, C_QK + o) for o in (0, 256)]
    items += [to_pv(o) for o in (0, 256)]
    items += [to_p(R_CONV + o, C_CONV + o) for o in range(0, 4 * D_CONV, 256)]
    items += [to_p(R_GATE + o, C_GATE + o) for o in (0, 256)]
    return items


def _finish_tile(p_ref, pv_ref, slot, wup_ref, bgk_ref, gng_ref, cw_ref, wo_ref, fg_ref,
                 y_ref, sout_ref, cout_ref, s_ref, sbd_ref, tail_ref, emit):
    g = _gate_log_decay(p_ref[slot, :, C_LR:C_LR + RANK], wup_ref, bgk_ref)
    emit()

    rt = _iota((TILE, TILE), 0)
    ct = _iota((TILE, TILE), 1)
    cmask = jnp.where(((rt >> 6) == (ct >> 6)) & (ct <= rt), 1.0, 0.0).astype(BF16)
    b = _masked_sum(cmask, g)
    emit()

    ar = _iota((HEADS * CHUNK, CHUNK), 0) & (CHUNK - 1)
    ac = _iota((HEADS * CHUNK, CHUNK), 1)
    causal = ac <= ar

    o_chunks = []
    for c in range(TILE // CHUNK):
        r0 = c * CHUNK
        bc = b[r0:r0 + CHUNK]
        qc = p_ref[slot, r0:r0 + CHUNK, C_QK:C_QK + D_QK] * Q_SCALE
        kc = p_ref[slot, r0:r0 + CHUNK, C_QK + D_QK:C_GATE]
        vc = pv_ref[slot, r0:r0 + CHUNK, :]
        bmid = bc[CHUNK // 2:CHUNK // 2 + 1]
        blast = bc[CHUNK - 1:CHUNK]
        q_in = (qc * jnp.exp(bc)).astype(BF16)
        q_a = qc * jnp.exp(bc - bmid)
        k_a = (kc * jnp.exp(bmid - bc)).astype(BF16)
        k_d = (kc * jnp.exp(blast - bc)).astype(BF16)

        a_all = _dot_nt(_head_stack(q_a).astype(BF16), k_a)
        a_all = jnp.where(causal, a_all, 0.0).astype(BF16)
        inter = _dot(q_in, sbd_ref[...])
        intra = jnp.concatenate(
            [_dot(a_all[hd * CHUNK:(hd + 1) * CHUNK], vc[:, hd * DV:(hd + 1) * DV])
             for hd in range(HEADS)], axis=1)
        o_chunks.append(inter + intra)

        u_all = _dot_tn(k_d, vc)
        dec = jnp.transpose(jnp.broadcast_to(jnp.exp(blast), (DV, D_QK)))
        for hd in range(HEADS):
            rows = slice(hd * DK, (hd + 1) * DK)
            cols = slice(hd * DV, (hd + 1) * DV)
            s_new = s_ref[hd] * dec[rows] + u_all[rows, cols]
            s_ref[hd] = s_new
            sbd_ref[rows, cols] = s_new.astype(BF16)
        emit()

    o = jnp.concatenate(o_chunks, axis=0)
    mix = []
    for hd in range(HEADS):
        cols = slice(hd * DV, (hd + 1) * DV)
        gate = p_ref[slot, :, C_GATE + hd * DV:C_GATE + (hd + 1) * DV]
        mix.append((_rmsnorm(o[:, cols], gng_ref[...]) * _silu(gate)).astype(BF16))
        emit()

    cw = cw_ref[...]
    row = _iota((TILE, 256), 0)
    for half in range(D_CONV // 256):
        cols = slice(half * 256, (half + 1) * 256)

        def conv_in(k, cols=cols):
            return p_ref[slot, :, C_CONV + k * D_CONV + cols.start:C_CONV + k * D_CONV + cols.stop]

        hc = conv_in(2) * conv_in(0)
        tail = tail_ref[:, cols]
        h1 = jnp.where(row == 0, tail[7:8], pltpu.roll(hc, 1, 0))
        h2 = jnp.where(row == 0, tail[6:7], jnp.where(row == 1, tail[7:8], pltpu.roll(hc, 2, 0)))
        yc = cw[0:1, cols] * h2 + cw[1:2, cols] * h1 + cw[2:3, cols] * hc
        mix.append((conv_in(1) * yc * _silu(conv_in(3))).astype(BF16))
        tail_ref[:, cols] = hc[TILE - 8:TILE]
        cout_ref[:, cols] = hc[TILE - 2:TILE]
        emit()

    out = p_ref[slot, :, C_X:C_QK] + _dot(jnp.concatenate(mix, axis=1), wo_ref[...])
    emit()
    half_rows = TILE // 2
    y_ref[0:half_rows] = _rmsnorm(out[0:half_rows], fg_ref[...])
    emit()
    y_ref[half_rows:TILE] = _rmsnorm(out[half_rows:TILE], fg_ref[...])
    sout_ref[...] = s_ref[...]


def _prompt_kernel(tiles_per_seq, x_ref, ng_ref, wt_ref, wup_ref, bgk_ref, gng_ref, cw_ref,
                   wo_ref, fg_ref,
                   y_ref, sout_ref, cout_ref,
                   p_ref, pv_ref, s_ref, sbd_ref, tail_ref):
    step = pl.program_id(0)

    @pl.when(step == 0)
    def _():
        p_ref[1] = jnp.zeros(p_ref.shape[1:], F32)
        pv_ref[1] = jnp.zeros(pv_ref.shape[1:], BF16)

    @pl.when(jnp.logical_or(step == 0, (step - 1) % tiles_per_seq == 0))
    def _():
        s_ref[...] = jnp.zeros_like(s_ref)
        sbd_ref[...] = jnp.zeros_like(sbd_ref)
        tail_ref[...] = jnp.zeros_like(tail_ref)

    def body(write_slot, read_slot):
        items = _project_items(x_ref, ng_ref, wt_ref, p_ref, pv_ref, write_slot)
        counts = iter(ITEM_SCHEDULE)
        items[0]()
        pending = iter(items[1:])

        def emit():
            for _ in range(next(counts)):
                next(pending)()

        _finish_tile(p_ref, pv_ref, read_slot, wup_ref, bgk_ref, gng_ref, cw_ref, wo_ref, fg_ref,
                     y_ref, sout_ref, cout_ref, s_ref, sbd_ref, tail_ref, emit)
        assert next(counts, None) is None and next(pending, None) is None

    @pl.when(step % 2 == 0)
    def _():
        body(0, 1)

    @pl.when(step % 2 == 1)
    def _():
        body(1, 0)


def _sample_kernel(x_ref, cprev_ref, sin_ref, ng_ref, w_ref, wup_ref, bgk_ref,
                   gng_ref, cw_ref, wo_ref, fg_ref,
                   y_ref, sout_ref, cout_ref):
    nt = SEQ_BLK * DEC_LEN
    gt = GRP * DEC_LEN
    n_grp = SEQ_BLK // GRP

    def seqs(g):
        return slice(g * GRP, (g + 1) * GRP)

    x = jnp.concatenate([x_ref[seqs(g), t, :] for g in range(n_grp) for t in range(DEC_LEN)], axis=0)
    h = _rmsnorm(x, ng_ref[...]).astype(BF16)

    qk = _proj(h, w_ref, R_QK, R_V)
    v = _proj(h, w_ref, R_V, R_GATE).astype(BF16)
    gate = _proj(h, w_ref, R_GATE, R_CONV)
    g = _gate_log_decay(_proj(h, w_ref, R_LR, R_LR + RANK), wup_ref, bgk_ref)

    def tok(i):
        return (i >> 4) & (DEC_LEN - 1)

    def same_seq(r, c):
        return ((r >> 6) == (c >> 6)) & ((r & (GRP - 1)) == (c & (GRP - 1)))

    rt = _iota((nt, nt), 0)
    ct = _iota((nt, nt), 1)
    same = same_seq(rt, ct)
    cmask = jnp.where(same & (tok(ct) <= tok(rt)), 1.0, 0.0).astype(BF16)
    fmask = jnp.where(same, 1.0, 0.0).astype(BF16)
    b = _masked_sum(cmask, g)
    bl = _masked_sum(fmask, g)

    q = qk[:, 0:D_QK] * Q_SCALE
    k = qk[:, D_QK:2 * D_QK]
    q_in = (q * jnp.exp(b)).astype(BF16)
    k_a = (k * jnp.exp(-b)).astype(BF16)
    k_d = k * jnp.exp(bl - b)
    k_dt = jnp.transpose(k_d).astype(BF16)
    dec_t = jnp.transpose(jnp.exp(bl))

    ar = _iota((HEADS * gt, gt), 0) & (gt - 1)
    ac = _iota((HEADS * gt, gt), 1)
    amask = same_seq(ar, ac) & (tok(ac) <= tok(ar))
    ir = _iota((HEADS * gt, GRP * D_QK), 0)
    ic = _iota((HEADS * gt, GRP * D_QK), 1)
    imask = ((ir & (GRP - 1)) == (ic >> 8)) & ((ir >> 6) == ((ic >> 6) & 3))

    o_groups = []
    for gi in range(n_grp):
        r0 = gi * gt
        qg = q_in[r0:r0 + gt]
        a_all = _dot_nt(_head_stack(qg), k_a[r0:r0 + gt])
        a_all = jnp.where(amask, a_all, 0.0).astype(BF16)
        s_cat = sin_ref[seqs(gi)].reshape(GRP * D_QK, DV).astype(BF16)
        lhs = jnp.where(imask, jnp.tile(qg, (HEADS, GRP)), 0.0)
        inter = _dot(lhs, s_cat)
        outs = []
        for hd in range(HEADS):
            rows = slice(hd * gt, (hd + 1) * gt)
            intra = _dot(a_all[rows], v[r0:r0 + gt, hd * DV:(hd + 1) * DV])
            outs.append(inter[rows] + intra)
        o_groups.append(jnp.concatenate(outs, axis=1))
    o = _gla_epilogue(jnp.concatenate(o_groups, axis=0), gate, gng_ref)

    ur = _iota((SEQ_BLK * DK, nt), 0) >> 6
    uc = _iota((SEQ_BLK * DK, nt), 1)
    umask = ur == (((uc >> 6) << 4) | (uc & (GRP - 1)))
    for hd in range(HEADS):
        kt = jnp.tile(k_dt[hd * DK:(hd + 1) * DK], (SEQ_BLK, 1))
        u_h = _dot(jnp.where(umask, kt, 0.0), v[:, hd * DV:(hd + 1) * DV])
        for j in range(SEQ_BLK):
            col = (j // GRP) * gt + j % GRP
            dec = jnp.broadcast_to(dec_t[hd * DK:(hd + 1) * DK, col:col + 1], (DK, DV))
            sout_ref[j, hd] = sin_ref[j, hd] * dec + u_h[j * DK:(j + 1) * DK]

    conv = _proj(h, w_ref, R_CONV, R_LR)
    hc = conv[:, 2 * D_CONV:3 * D_CONV] * conv[:, 0:D_CONV]
    h1, h2 = [], []
    for gi in range(n_grp):
        r0 = gi * gt
        c0 = cprev_ref[seqs(gi), 0, :]
        c1 = cprev_ref[seqs(gi), 1, :]
        h1 += [c1, hc[r0:r0 + gt - GRP]]
        h2 += [c0, c1, hc[r0:r0 + gt - 2 * GRP]]
        cout_ref[seqs(gi), 0, :] = hc[r0 + gt - 2 * GRP:r0 + gt - GRP]
        cout_ref[seqs(gi), 1, :] = hc[r0 + gt - GRP:r0 + gt]
    cw = cw_ref[...]
    yc = cw[0:1] * jnp.concatenate(h2, axis=0) + cw[1:2] * jnp.concatenate(h1, axis=0) + cw[2:3] * hc
    yc = conv[:, D_CONV:2 * D_CONV] * yc * _silu(conv[:, 3 * D_CONV:4 * D_CONV])

    y = _out_proj(x, o, yc, wo_ref, fg_ref)
    for gi in range(n_grp):
        for t in range(DEC_LEN):
            r0 = gi * gt + t * GRP
            y_ref[seqs(gi), t, :] = y[r0:r0 + GRP]


def _const_spec(shape):
    return pl.BlockSpec(shape, lambda *_: (0,) * len(shape))


def _weight_specs():
    return [
        _const_spec((1, D_MODEL)),
        _const_spec((D_MODEL, N_W)),
        _const_spec((RANK, D_QK)),
        _const_spec((1, D_QK)),
        _const_spec((1, DV)),
        _const_spec((3, D_CONV)),
        _const_spec((D_MODEL, D_MODEL)),
        _const_spec((1, D_MODEL)),
    ]


def _prep_kernel(wt_ref, wo_ref, w_ref, wob_ref):
    w_ref[...] = jnp.transpose(wt_ref[...]).astype(BF16)
    wob_ref[...] = wo_ref[...].astype(BF16)


def _prepare_weights(w_in, w_out):
    n_qkvg = SRC_LR // PREP_BLK
    n_conv = (4 * D_CONV) // PREP_BLK
    n_blk = N_W // PREP_BLK
    assert n_blk == n_qkvg + n_conv + 1

    def src_row(i):
        row = jnp.where(i < n_qkvg, PREP_BLK * i,
                        jnp.where(i < n_qkvg + n_conv, SRC_CONV + PREP_BLK * (i - n_qkvg), SRC_LR))
        return pl.multiple_of(row, RANK)

    wo_rows = D_MODEL // n_blk
    return pl.pallas_call(
        _prep_kernel,
        grid=(n_blk,),
        in_specs=[
            pl.BlockSpec((pl.Element(PREP_BLK), pl.Element(D_MODEL)), lambda i: (src_row(i), 0)),
            pl.BlockSpec((wo_rows, D_MODEL), lambda i: (i, 0)),
        ],
        out_specs=[
            pl.BlockSpec((D_MODEL, PREP_BLK), lambda i: (0, i)),
            pl.BlockSpec((wo_rows, D_MODEL), lambda i: (i, 0)),
        ],
        out_shape=[
            jax.ShapeDtypeStruct((D_MODEL, N_W), BF16),
            jax.ShapeDtypeStruct((D_MODEL, D_MODEL), BF16),
        ],
        compiler_params=pltpu.CompilerParams(dimension_semantics=("arbitrary",)),
        name="weight_prep",
    )(jnp.swapaxes(w_in[0], 0, 1), w_out[0])


def kernel(x_prompt, x_sample, state_gla, state_conv, norm_gain, w_in, w_gk_up, b_gk,
           gla_norm_gain, conv_w, w_out, final_norm_gain):
    n_batch, seq_len, _ = x_prompt.shape
    n_dec = x_sample.shape[0]
    tiles_per_seq = seq_len // TILE
    n_tiles = n_batch * tiles_per_seq
    w_proj, w_o = _prepare_weights(w_in, w_out)
    weights = (
        norm_gain.reshape(1, D_MODEL),
        w_proj,
        w_gk_up[0].astype(BF16),
        b_gk.reshape(1, D_QK),
        gla_norm_gain.reshape(1, DV),
        conv_w[0],
        w_o,
        final_norm_gain.reshape(1, D_MODEL),
    )

    def in_tile(j):
        t = jnp.minimum(j, n_tiles - 1)
        return (t // tiles_per_seq, t % tiles_per_seq, 0)

    def out_tile(j):
        t = jnp.maximum(j - 1, 0)
        return (t // tiles_per_seq, t % tiles_per_seq, 0)

    def out_seq(j):
        return jnp.maximum(j - 1, 0) // tiles_per_seq

    y_p, s_p, c_p = pl.pallas_call(
        functools.partial(_prompt_kernel, tiles_per_seq),
        grid=(n_tiles + 1,),
        in_specs=[pl.BlockSpec((None, TILE, D_MODEL), in_tile)] + _weight_specs(),
        out_specs=[
            pl.BlockSpec((None, TILE, D_MODEL), out_tile),
            pl.BlockSpec((None, None, HEADS, DK, DV), lambda j: (0, out_seq(j), 0, 0, 0)),
            pl.BlockSpec((None, None, 2, D_CONV), lambda j: (0, out_seq(j), 0, 0)),
        ],
        out_shape=[
            jax.ShapeDtypeStruct((n_batch, seq_len, D_MODEL), F32),
            jax.ShapeDtypeStruct((1, n_batch, HEADS, DK, DV), F32),
            jax.ShapeDtypeStruct((1, n_batch, 2, D_CONV), F32),
        ],
        scratch_shapes=[
            pltpu.VMEM((2, TILE, C_END), F32),
            pltpu.VMEM((2, TILE, D_GLA), BF16),
            pltpu.VMEM((HEADS, DK, DV), F32),
            pltpu.VMEM((D_QK, D_GLA), BF16),
            pltpu.VMEM((8, D_CONV), F32),
        ],
        compiler_params=pltpu.CompilerParams(
            dimension_semantics=("arbitrary",), vmem_limit_bytes=VMEM_LIMIT),
        name="gla_conv_prompt",
    )(x_prompt, *weights)

    y_s, s_s, c_s = pl.pallas_call(
        _sample_kernel,
        grid=(n_dec // SEQ_BLK,),
        in_specs=[
            pl.BlockSpec((SEQ_BLK, DEC_LEN, D_MODEL), lambda i: (i, 0, 0)),
            pl.BlockSpec((None, SEQ_BLK, 2, D_CONV), lambda i: (0, i, 0, 0)),
            pl.BlockSpec((None, SEQ_BLK, HEADS, DK, DV), lambda i: (0, i, 0, 0, 0)),
        ] + _weight_specs(),
        out_specs=[
            pl.BlockSpec((SEQ_BLK, DEC_LEN, D_MODEL), lambda i: (i, 0, 0)),
            pl.BlockSpec((None, SEQ_BLK, HEADS, DK, DV), lambda i: (0, i, 0, 0, 0)),
            pl.BlockSpec((None, SEQ_BLK, 2, D_CONV), lambda i: (0, i, 0, 0)),
        ],
        out_shape=[
            jax.ShapeDtypeStruct((n_dec, DEC_LEN, D_MODEL), F32),
            jax.ShapeDtypeStruct((1, n_dec, HEADS, DK, DV), F32),
            jax.ShapeDtypeStruct((1, n_dec, 2, D_CONV), F32),
        ],
        compiler_params=pltpu.CompilerParams(
            dimension_semantics=("arbitrary",), vmem_limit_bytes=VMEM_LIMIT),
        name="gla_conv_sample",
    )(x_sample, state_conv, state_gla, *weights)

    return (y_p, y_s, s_p, c_p, s_s, c_s)
```

```python
import functools

import jax
import jax.numpy as jnp
from jax import lax
from jax.experimental import pallas as pl
from jax.experimental.pallas import tpu as pltpu

D_MODEL = 1024
HEADS = 4
DK = 64
DV = 128
D_QK = HEADS * DK
D_GLA = HEADS * DV
D_CONV = 512
RANK = 16
CHUNK = 64
TILE = 256
DEC_LEN = 4
SEQ_BLK = 32
GRP = 16
EPS = 1e-6
Q_SCALE = DK ** -0.5
GATE_SCALE = 1.0 / 16.0
VMEM_LIMIT = 52 * 1024 * 1024

R_QK = 0
R_V = 2 * D_QK
R_GATE = R_V + D_GLA
R_CONV = R_GATE + D_GLA
R_LR = R_CONV + 4 * D_CONV
PREP_BLK = 512
N_W = R_LR + PREP_BLK
SRC_LR = 2 * D_QK + 2 * D_GLA
SRC_CONV = SRC_LR + RANK

C_X = 0
C_QK = C_X + D_MODEL
C_GATE = C_QK + 2 * D_QK
C_CONV = C_GATE + D_GLA
C_LR = C_CONV + 4 * D_CONV
C_END = C_LR + 128

F32 = jnp.float32
BF16 = jnp.bfloat16


def _dot(a, b):
    return jnp.dot(a, b, preferred_element_type=F32)


def _dot_nt(a, b):
    return lax.dot_general(a, b, (((1,), (1,)), ((), ())), preferred_element_type=F32)


def _proj(h, w_ref, lo, hi):
    return _dot(h, w_ref[:, lo:hi])


def _dot_tn(a, b):
    return lax.dot_general(a, b, (((0,), (0,)), ((), ())), preferred_element_type=F32)


def _rmsnorm(x, gain):
    ms = jnp.mean(x * x, axis=-1, keepdims=True)
    return x * lax.rsqrt(ms + EPS) * gain


def _silu(x):
    return x * (1.0 / (1.0 + jnp.exp(-x)))


def _log_sigmoid(z):
    return jnp.minimum(z, 0.0) - jnp.log(1.0 + jnp.exp(-jnp.abs(z)))


def _iota(shape, dim):
    return lax.broadcasted_iota(jnp.int32, shape, dim)


def _masked_sum(mask_bf16, g):
    g1 = g.astype(BF16)
    r1 = g - g1.astype(F32)
    g2 = r1.astype(BF16)
    g3 = (r1 - g2.astype(F32)).astype(BF16)
    return _dot(mask_bf16, g1) + _dot(mask_bf16, g2) + _dot(mask_bf16, g3)


def _head_stack(q):
    lane_head = _iota(q.shape, 1) >> 6
    return jnp.concatenate([jnp.where(lane_head == h, q, 0.0) for h in range(HEADS)], axis=0)


def _gate_log_decay(lr, wup_ref, bgk_ref):
    z = _dot(lr.astype(BF16), wup_ref[...]) + bgk_ref[...]
    return _log_sigmoid(z) * GATE_SCALE


def _gla_epilogue(o, gate, gng_ref):
    outs = []
    for hd in range(HEADS):
        oh = o[:, hd * DV:(hd + 1) * DV]
        outs.append(_rmsnorm(oh, gng_ref[...]))
    return jnp.concatenate(outs, axis=1) * _silu(gate)


def _out_proj(x, o, yc, wo_ref, fg_ref):
    mix = jnp.concatenate([o, yc], axis=1).astype(BF16)
    out = x + _dot(mix, wo_ref[...])
    return _rmsnorm(out, fg_ref[...])


ITEM_SCHEDULE = (0, 5, 0, 0, 0, 0, 0, 0, 0, 0, 0, 6, 3, 1)


def _project_items(x_ref, ng_ref, wt_ref, p_ref, pv_ref, slot):
    cache = {}

    def norm():
        x = x_ref[...]
        cache["h"] = _rmsnorm(x, ng_ref[...]).astype(BF16)
        p_ref[slot, :, C_X:C_QK] = x

    def to_p(row, col, width=256):
        def item():
            p_ref[slot, :, col:col + width] = _proj(cache["h"], wt_ref, row, row + width)
        return item

    def to_pv(off):
        def item():
            pv_ref[slot, :, off:off + 256] = _proj(
                cache["h"], wt_ref, R_V + off, R_V + off + 256).astype(BF16)
        return item

    items = [norm, to_p(R_LR, C_LR, RANK)]
    items += [to_p(R_QK + o, C_QK + o) for o in (0, 256)]
    items += [to_pv(o) for o in (0, 256)]
    items += [to_p(R_CONV + o, C_CONV + o) for o in range(0, 4 * D_CONV, 256)]
    items += [to_p(R_GATE + o, C_GATE + o) for o in (0, 256)]
    return items


def _finish_tile(p_ref, pv_ref, slot, wup_ref, bgk_ref, gng_ref, cw_ref, wo_ref, fg_ref,
                 y_ref, sout_ref, cout_ref, s_ref, sbd_ref, tail_ref, emit):
    g = _gate_log_decay(p_ref[slot, :, C_LR:C_LR + RANK], wup_ref, bgk_ref)
    emit()

    rt = _iota((TILE, TILE), 0)
    ct = _iota((TILE, TILE), 1)
    cmask = jnp.where(((rt >> 6) == (ct >> 6)) & (ct <= rt), 1.0, 0.0).astype(BF16)
    b = _masked_sum(cmask, g)
    emit()

    nchunk = TILE // CHUNK
    blast_rows = jnp.concatenate(
        [jnp.broadcast_to(b[c * CHUNK + CHUNK - 1:(c + 1) * CHUNK], (CHUNK, D_QK))
         for c in range(nchunk)], axis=0)
    k_all = p_ref[slot, :, C_QK + D_QK:C_GATE]
    k_d = k_all * jnp.exp(blast_rows - b)
    v_all = pv_ref[slot]
    tok_chunk = _iota((TILE, 2 * DK), 0) >> 6
    incr = []
    for pair in range(HEADS // 2):
        kp = k_d[:, pair * 2 * DK:(pair + 1) * 2 * DK]
        lhs_t = jnp.concatenate(
            [jnp.where(tok_chunk == c, kp, 0.0) for c in range(nchunk)], axis=1).astype(BF16)
        incr.append(_dot_tn(lhs_t, v_all[:, pair * 2 * DV:(pair + 1) * 2 * DV]))

    ar = _iota((HEADS * CHUNK, CHUNK), 0) & (CHUNK - 1)
    ac = _iota((HEADS * CHUNK, CHUNK), 1)
    causal = ac <= ar

    state = [s_ref[hd] for hd in range(HEADS)]
    o_chunks = []
    for c in range(nchunk):
        r0 = c * CHUNK
        bc = b[r0:r0 + CHUNK]
        qc = p_ref[slot, r0:r0 + CHUNK, C_QK:C_QK + D_QK] * Q_SCALE
        kc = p_ref[slot, r0:r0 + CHUNK, C_QK + D_QK:C_GATE]
        vc = pv_ref[slot, r0:r0 + CHUNK, :]
        bmid = bc[CHUNK // 2:CHUNK // 2 + 1]
        blast = bc[CHUNK - 1:CHUNK]
        q_in = (qc * jnp.exp(bc)).astype(BF16)
        q_a = qc * jnp.exp(bc - bmid)
        k_a = (kc * jnp.exp(bmid - bc)).astype(BF16)

        a_all = _dot_nt(_head_stack(q_a).astype(BF16), k_a)
        a_all = jnp.where(causal, a_all, 0.0).astype(BF16)
        for hd in range(HEADS):
            sbd_ref[c, hd * DK:(hd + 1) * DK, hd * DV:(hd + 1) * DV] = state[hd].astype(BF16)
        inter = _dot(q_in, sbd_ref[c])
        intra = jnp.concatenate(
            [_dot(a_all[hd * CHUNK:(hd + 1) * CHUNK], vc[:, hd * DV:(hd + 1) * DV])
             for hd in range(HEADS)], axis=1)
        o_chunks.append(inter + intra)

        dec = jnp.transpose(jnp.broadcast_to(jnp.exp(blast), (DV, D_QK)))
        for hd in range(HEADS):
            rows = slice(hd * DK, (hd + 1) * DK)
            pair, sub = divmod(hd, 2)
            i0 = c * 2 * DK + sub * DK
            state[hd] = state[hd] * dec[rows] + incr[pair][i0:i0 + DK, sub * DV:(sub + 1) * DV]
        emit()
    for hd in range(HEADS):
        s_ref[hd] = state[hd]
        sout_ref[hd] = state[hd]

    o = jnp.concatenate(o_chunks, axis=0)
    mix = []
    for hd in range(HEADS):
        cols = slice(hd * DV, (hd + 1) * DV)
        gate = p_ref[slot, :, C_GATE + hd * DV:C_GATE + (hd + 1) * DV]
        mix.append((_rmsnorm(o[:, cols], gng_ref[...]) * _silu(gate)).astype(BF16))
        emit()

    cw = cw_ref[...]
    row = _iota((TILE, 256), 0)
    for half in range(D_CONV // 256):
        cols = slice(half * 256, (half + 1) * 256)

        def conv_in(k, cols=cols):
            return p_ref[slot, :, C_CONV + k * D_CONV + cols.start:C_CONV + k * D_CONV + cols.stop]

        hc = conv_in(2) * conv_in(0)
        tail = tail_ref[:, cols]
        h1 = jnp.where(row == 0, tail[7:8], pltpu.roll(hc, 1, 0))
        h2 = jnp.where(row == 0, tail[6:7], jnp.where(row == 1, tail[7:8], pltpu.roll(hc, 2, 0)))
        yc = cw[0:1, cols] * h2 + cw[1:2, cols] * h1 + cw[2:3, cols] * hc
        mix.append((conv_in(1) * yc * _silu(conv_in(3))).astype(BF16))
        tail_ref[:, cols] = hc[TILE - 8:TILE]
        cout_ref[:, cols] = hc[TILE - 2:TILE]
        emit()

    out = p_ref[slot, :, C_X:C_QK] + _dot(jnp.concatenate(mix, axis=1), wo_ref[...])
    emit()
    half_rows = TILE // 2
    y_ref[0:half_rows] = _rmsnorm(out[0:half_rows], fg_ref[...])
    emit()
    y_ref[half_rows:TILE] = _rmsnorm(out[half_rows:TILE], fg_ref[...])


def _prompt_kernel(tiles_per_seq, x_ref, ng_ref, wt_ref, wup_ref, bgk_ref, gng_ref, cw_ref,
                   wo_ref, fg_ref,
                   y_ref, sout_ref, cout_ref,
                   p_ref, pv_ref, s_ref, sbd_ref, tail_ref):
    step = pl.program_id(0)

    @pl.when(step == 0)
    def _():
        p_ref[1] = jnp.zeros(p_ref.shape[1:], F32)
        pv_ref[1] = jnp.zeros(pv_ref.shape[1:], BF16)
        sbd_ref[...] = jnp.zeros_like(sbd_ref)

    @pl.when(jnp.logical_or(step == 0, (step - 1) % tiles_per_seq == 0))
    def _():
        s_ref[...] = jnp.zeros_like(s_ref)
        tail_ref[...] = jnp.zeros_like(tail_ref)

    def body(write_slot, read_slot):
        items = _project_items(x_ref, ng_ref, wt_ref, p_ref, pv_ref, write_slot)
        counts = iter(ITEM_SCHEDULE)
        items[0]()
        pending = iter(items[1:])

        def emit():
            for _ in range(next(counts)):
                next(pending)()

        _finish_tile(p_ref, pv_ref, read_slot, wup_ref, bgk_ref, gng_ref, cw_ref, wo_ref, fg_ref,
                     y_ref, sout_ref, cout_ref, s_ref, sbd_ref, tail_ref, emit)
        assert next(counts, None) is None and next(pending, None) is None

    @pl.when(step % 2 == 0)
    def _():
        body(0, 1)

    @pl.when(step % 2 == 1)
    def _():
        body(1, 0)


def _sample_kernel(x_ref, cprev_ref, sin_ref, ng_ref, w_ref, wup_ref, bgk_ref,
                   gng_ref, cw_ref, wo_ref, fg_ref,
                   y_ref, sout_ref, cout_ref):
    nt = SEQ_BLK * DEC_LEN
    gt = GRP * DEC_LEN
    n_grp = SEQ_BLK // GRP

    def seqs(g):
        return slice(g * GRP, (g + 1) * GRP)

    x = jnp.concatenate([x_ref[seqs(g), t, :] for g in range(n_grp) for t in range(DEC_LEN)], axis=0)
    h = _rmsnorm(x, ng_ref[...]).astype(BF16)

    qk = _proj(h, w_ref, R_QK, R_V)
    v = _proj(h, w_ref, R_V, R_GATE).astype(BF16)
    gate = _proj(h, w_ref, R_GATE, R_CONV)
    g = _gate_log_decay(_proj(h, w_ref, R_LR, R_LR + RANK), wup_ref, bgk_ref)

    def tok(i):
        return (i >> 4) & (DEC_LEN - 1)

    def same_seq(r, c):
        return ((r >> 6) == (c >> 6)) & ((r & (GRP - 1)) == (c & (GRP - 1)))

    rt = _iota((nt, nt), 0)
    ct = _iota((nt, nt), 1)
    same = same_seq(rt, ct)
    cmask = jnp.where(same & (tok(ct) <= tok(rt)), 1.0, 0.0).astype(BF16)
    fmask = jnp.where(same, 1.0, 0.0).astype(BF16)
    b = _masked_sum(cmask, g)
    bl = _masked_sum(fmask, g)

    q = qk[:, 0:D_QK] * Q_SCALE
    k = qk[:, D_QK:2 * D_QK]
    q_in = (q * jnp.exp(b)).astype(BF16)
    k_a = (k * jnp.exp(-b)).astype(BF16)
    k_d = k * jnp.exp(bl - b)
    k_dt = jnp.transpose(k_d).astype(BF16)
    dec_t = jnp.transpose(jnp.exp(bl))

    ar = _iota((HEADS * gt, gt), 0) & (gt - 1)
    ac = _iota((HEADS * gt, gt), 1)
    amask = same_seq(ar, ac) & (tok(ac) <= tok(ar))
    ir = _iota((HEADS * gt, GRP * D_QK), 0)
    ic = _iota((HEADS * gt, GRP * D_QK), 1)
    imask = ((ir & (GRP - 1)) == (ic >> 8)) & ((ir >> 6) == ((ic >> 6) & 3))

    o_groups = []
    for gi in range(n_grp):
        r0 = gi * gt
        qg = q_in[r0:r0 + gt]
        a_all = _dot_nt(_head_stack(qg), k_a[r0:r0 + gt])
        a_all = jnp.where(amask, a_all, 0.0).astype(BF16)
        s_cat = sin_ref[seqs(gi)].reshape(GRP * D_QK, DV).astype(BF16)
        lhs = jnp.where(imask, jnp.tile(qg, (HEADS, GRP)), 0.0)
        inter = _dot(lhs, s_cat)
        outs = []
        for hd in range(HEADS):
            rows = slice(hd * gt, (hd + 1) * gt)
            intra = _dot(a_all[rows], v[r0:r0 + gt, hd * DV:(hd + 1) * DV])
            outs.append(inter[rows] + intra)
        o_groups.append(jnp.concatenate(outs, axis=1))
    o = _gla_epilogue(jnp.concatenate(o_groups, axis=0), gate, gng_ref)

    ur = _iota((SEQ_BLK * DK, nt), 0) >> 6
    uc = _iota((SEQ_BLK * DK, nt), 1)
    umask = ur == (((uc >> 6) << 4) | (uc & (GRP - 1)))
    for hd in range(HEADS):
        kt = jnp.tile(k_dt[hd * DK:(hd + 1) * DK], (SEQ_BLK, 1))
        u_h = _dot(jnp.where(umask, kt, 0.0), v[:, hd * DV:(hd + 1) * DV])
        for j in range(SEQ_BLK):
            col = (j // GRP) * gt + j % GRP
            dec = jnp.broadcast_to(dec_t[hd * DK:(hd + 1) * DK, col:col + 1], (DK, DV))
            sout_ref[j, hd] = sin_ref[j, hd] * dec + u_h[j * DK:(j + 1) * DK]

    conv = _proj(h, w_ref, R_CONV, R_LR)
    hc = conv[:, 2 * D_CONV:3 * D_CONV] * conv[:, 0:D_CONV]
    h1, h2 = [], []
    for gi in range(n_grp):
        r0 = gi * gt
        c0 = cprev_ref[seqs(gi), 0, :]
        c1 = cprev_ref[seqs(gi), 1, :]
        h1 += [c1, hc[r0:r0 + gt - GRP]]
        h2 += [c0, c1, hc[r0:r0 + gt - 2 * GRP]]
        cout_ref[seqs(gi), 0, :] = hc[r0 + gt - 2 * GRP:r0 + gt - GRP]
        cout_ref[seqs(gi), 1, :] = hc[r0 + gt - GRP:r0 + gt]
    cw = cw_ref[...]
    yc = cw[0:1] * jnp.concatenate(h2, axis=0) + cw[1:2] * jnp.concatenate(h1, axis=0) + cw[2:3] * hc
    yc = conv[:, D_CONV:2 * D_CONV] * yc * _silu(conv[:, 3 * D_CONV:4 * D_CONV])

    y = _out_proj(x, o, yc, wo_ref, fg_ref)
    for gi in range(n_grp):
        for t in range(DEC_LEN):
            r0 = gi * gt + t * GRP
            y_ref[seqs(gi), t, :] = y[r0:r0 + GRP]


def _const_spec(shape):
    return pl.BlockSpec(shape, lambda *_: (0,) * len(shape))


def _weight_specs():
    return [
        _const_spec((1, D_MODEL)),
        _const_spec((D_MODEL, N_W)),
        _const_spec((RANK, D_QK)),
        _const_spec((1, D_QK)),
        _const_spec((1, DV)),
        _const_spec((3, D_CONV)),
        _const_spec((D_MODEL, D_MODEL)),
        _const_spec((1, D_MODEL)),
    ]


def _prep_kernel(wt_ref, wo_ref, w_ref, wob_ref):
    w_ref[...] = jnp.transpose(wt_ref[...]).astype(BF16)
    wob_ref[...] = wo_ref[...].astype(BF16)


def _prepare_weights(w_in, w_out):
    n_qkvg = SRC_LR // PREP_BLK
    n_conv = (4 * D_CONV) // PREP_BLK
    n_blk = N_W // PREP_BLK
    assert n_blk == n_qkvg + n_conv + 1

    def src_row(i):
        row = jnp.where(i < n_qkvg, PREP_BLK * i,
                        jnp.where(i < n_qkvg + n_conv, SRC_CONV + PREP_BLK * (i - n_qkvg), SRC_LR))
        return pl.multiple_of(row, RANK)

    wo_rows = D_MODEL // n_blk
    return pl.pallas_call(
        _prep_kernel,
        grid=(n_blk,),
        in_specs=[
            pl.BlockSpec((pl.Element(PREP_BLK), pl.Element(D_MODEL)), lambda i: (src_row(i), 0)),
            pl.BlockSpec((wo_rows, D_MODEL), lambda i: (i, 0)),
        ],
        out_specs=[
            pl.BlockSpec((D_MODEL, PREP_BLK), lambda i: (0, i)),
            pl.BlockSpec((wo_rows, D_MODEL), lambda i: (i, 0)),
        ],
        out_shape=[
            jax.ShapeDtypeStruct((D_MODEL, N_W), BF16),
            jax.ShapeDtypeStruct((D_MODEL, D_MODEL), BF16),
        ],
        compiler_params=pltpu.CompilerParams(dimension_semantics=("arbitrary",)),
        name="weight_prep",
    )(jnp.swapaxes(w_in[0], 0, 1), w_out[0])


def kernel(x_prompt, x_sample, state_gla, state_conv, norm_gain, w_in, w_gk_up, b_gk,
           gla_norm_gain, conv_w, w_out, final_norm_gain):
    n_batch, seq_len, _ = x_prompt.shape
    n_dec = x_sample.shape[0]
    tiles_per_seq = seq_len // TILE
    n_tiles = n_batch * tiles_per_seq
    w_proj, w_o = _prepare_weights(w_in, w_out)
    weights = (
        norm_gain.reshape(1, D_MODEL),
        w_proj,
        w_gk_up[0].astype(BF16),
        b_gk.reshape(1, D_QK),
        gla_norm_gain.reshape(1, DV),
        conv_w[0],
        w_o,
        final_norm_gain.reshape(1, D_MODEL),
    )

    def in_tile(j):
        t = jnp.minimum(j, n_tiles - 1)
        return (t // tiles_per_seq, t % tiles_per_seq, 0)

    def out_tile(j):
        t = jnp.maximum(j - 1, 0)
        return (t // tiles_per_seq, t % tiles_per_seq, 0)

    def out_seq(j):
        return jnp.maximum(j - 1, 0) // tiles_per_seq

    y_p, s_p, c_p = pl.pallas_call(
        functools.partial(_prompt_kernel, tiles_per_seq),
        grid=(n_tiles + 1,),
        in_specs=[pl.BlockSpec((None, TILE, D_MODEL), in_tile)] + _weight_specs(),
        out_specs=[
            pl.BlockSpec((None, TILE, D_MODEL), out_tile),
            pl.BlockSpec((None, None, HEADS, DK, DV), lambda j: (0, out_seq(j), 0, 0, 0)),
            pl.BlockSpec((None, None, 2, D_CONV), lambda j: (0, out_seq(j), 0, 0)),
        ],
        out_shape=[
            jax.ShapeDtypeStruct((n_batch, seq_len, D_MODEL), F32),
            jax.ShapeDtypeStruct((1, n_batch, HEADS, DK, DV), F32),
            jax.ShapeDtypeStruct((1, n_batch, 2, D_CONV), F32),
        ],
        scratch_shapes=[
            pltpu.VMEM((2, TILE, C_END), F32),
            pltpu.VMEM((2, TILE, D_GLA), BF16),
            pltpu.VMEM((HEADS, DK, DV), F32),
            pltpu.VMEM((TILE // CHUNK, D_QK, D_GLA), BF16),
            pltpu.VMEM((8, D_CONV), F32),
        ],
        compiler_params=pltpu.CompilerParams(
            dimension_semantics=("arbitrary",), vmem_limit_bytes=VMEM_LIMIT),
        name="gla_conv_prompt",
    )(x_prompt, *weights)

    y_s, s_s, c_s = pl.pallas_call(
        _sample_kernel,
        grid=(n_dec // SEQ_BLK,),
        in_specs=[
            pl.BlockSpec((SEQ_BLK, DEC_LEN, D_MODEL), lambda i: (i, 0, 0)),
            pl.BlockSpec((None, SEQ_BLK, 2, D_CONV), lambda i: (0, i, 0, 0)),
            pl.BlockSpec((None, SEQ_BLK, HEADS, DK, DV), lambda i: (0, i, 0, 0, 0)),
        ] + _weight_specs(),
        out_specs=[
            pl.BlockSpec((SEQ_BLK, DEC_LEN, D_MODEL), lambda i: (i, 0, 0)),
            pl.BlockSpec((None, SEQ_BLK, HEADS, DK, DV), lambda i: (0, i, 0, 0, 0)),
            pl.BlockSpec((None, SEQ_BLK, 2, D_CONV), lambda i: (0, i, 0, 0)),
        ],
        out_shape=[
            jax.ShapeDtypeStruct((n_dec, DEC_LEN, D_MODEL), F32),
            jax.ShapeDtypeStruct((1, n_dec, HEADS, DK, DV), F32),
            jax.ShapeDtypeStruct((1, n_dec, 2, D_CONV), F32),
        ],
        compiler_params=pltpu.CompilerParams(
            dimension_semantics=("arbitrary",), vmem_limit_bytes=VMEM_LIMIT),
        name="gla_conv_sample",
    )(x_sample, state_conv, state_gla, *weights)

    return (y_p, y_s, s_p, c_p, s_s, c_s)
```

```python
import functools

import jax
import jax.numpy as jnp
from jax import lax
from jax.experimental import pallas as pl
from jax.experimental.pallas import tpu as pltpu

D_MODEL = 1024
HEADS = 4
DK = 64
DV = 128
D_QK = HEADS * DK
D_GLA = HEADS * DV
D_CONV = 512
RANK = 16
CHUNK = 64
TILE = 256
DEC_LEN = 4
SEQ_BLK = 32
GRP = 16
EPS = 1e-6
Q_SCALE = DK ** -0.5
GATE_SCALE = 1.0 / 16.0
VMEM_LIMIT = 52 * 1024 * 1024

R_QK = 0
R_V = 2 * D_QK
R_GATE = R_V + D_GLA
R_CONV = R_GATE + D_GLA
R_LR = R_CONV + 4 * D_CONV
PREP_BLK = 512
N_W = R_LR + PREP_BLK
SRC_LR = 2 * D_QK + 2 * D_GLA
SRC_CONV = SRC_LR + RANK

C_X = 0
C_QK = C_X + D_MODEL
C_GATE = C_QK + 2 * D_QK
C_CONV = C_GATE + D_GLA
C_LR = C_CONV + 4 * D_CONV
C_END = C_LR + 128

F32 = jnp.float32
BF16 = jnp.bfloat16


def _dot(a, b):
    return jnp.dot(a, b, preferred_element_type=F32)


def _dot_nt(a, b):
    return lax.dot_general(a, b, (((1,), (1,)), ((), ())), preferred_element_type=F32)


def _proj(h, w_ref, lo, hi):
    return _dot(h, w_ref[:, lo:hi])


def _dot_tn(a, b):
    return lax.dot_general(a, b, (((0,), (0,)), ((), ())), preferred_element_type=F32)


def _rmsnorm(x, gain):
    ms = jnp.mean(x * x, axis=-1, keepdims=True)
    return x * lax.rsqrt(ms + EPS) * gain


def _silu(x):
    return x * (1.0 / (1.0 + jnp.exp(-x)))


def _log_sigmoid(z):
    return jnp.minimum(z, 0.0) - jnp.log(1.0 + jnp.exp(-jnp.abs(z)))


def _iota(shape, dim):
    return lax.broadcasted_iota(jnp.int32, shape, dim)


def _masked_sum(mask_bf16, g):
    g1 = g.astype(BF16)
    r1 = g - g1.astype(F32)
    g2 = r1.astype(BF16)
    g3 = (r1 - g2.astype(F32)).astype(BF16)
    return _dot(mask_bf16, g1) + _dot(mask_bf16, g2) + _dot(mask_bf16, g3)


def _head_stack(q):
    lane_head = _iota(q.shape, 1) >> 6
    return jnp.concatenate([jnp.where(lane_head == h, q, 0.0) for h in range(HEADS)], axis=0)


def _gate_log_decay(lr, wup_ref, bgk_ref):
    z = _dot(lr.astype(BF16), wup_ref[...]) + bgk_ref[...]
    return _log_sigmoid(z) * GATE_SCALE


def _gla_epilogue(o, gate, gng_ref):
    outs = []
    for hd in range(HEADS):
        oh = o[:, hd * DV:(hd + 1) * DV]
        outs.append(_rmsnorm(oh, gng_ref[...]))
    return jnp.concatenate(outs, axis=1) * _silu(gate)


def _out_proj(x, o, yc, wo_ref, fg_ref):
    mix = jnp.concatenate([o, yc], axis=1).astype(BF16)
    out = x + _dot(mix, wo_ref[...])
    return _rmsnorm(out, fg_ref[...])


ITEM_SCHEDULE = (0, 5, 0, 0, 0, 0, 0, 0, 0, 0, 0, 6, 3, 1)


def _project_items(x_ref, ng_ref, wt_ref, p_ref, pv_ref, slot):
    cache = {}

    def norm():
        x = x_ref[...]
        cache["h"] = _rmsnorm(x, ng_ref[...]).astype(BF16)
        p_ref[slot, :, C_X:C_QK] = x

    def to_p(row, col, width=256):
        def item():
            p_ref[slot, :, col:col + width] = _proj(cache["h"], wt_ref, row, row + width)
        return item

    def to_pv(off):
        def item():
            pv_ref[slot, :, off:off + 256] = _proj(
                cache["h"], wt_ref, R_V + off, R_V + off + 256).astype(BF16)
        return item

    items = [norm, to_p(R_LR, C_LR, RANK)]
    items += [to_p(R_QK + o, C_QK + o) for o in (0, 256)]
    items += [to_pv(o) for o in (0, 256)]
    items += [to_p(R_CONV + o, C_CONV + o) for o in range(0, 4 * D_CONV, 256)]
    items += [to_p(R_GATE + o, C_GATE + o) for o in (0, 256)]
    return items


def _finish_tile(p_ref, pv_ref, slot, wup_ref, bgk_ref, gng_ref, cw_ref, wo_ref, fg_ref,
                 y_ref, cout_ref, sbd_ref, state, tail, emit):
    g = _gate_log_decay(p_ref[slot, :, C_LR:C_LR + RANK], wup_ref, bgk_ref)
    emit()

    rt = _iota((TILE, TILE), 0)
    ct = _iota((TILE, TILE), 1)
    cmask = jnp.where(((rt >> 6) == (ct >> 6)) & (ct <= rt), 1.0, 0.0).astype(BF16)
    b = _masked_sum(cmask, g)
    emit()

    nchunk = TILE // CHUNK
    blast_rows = jnp.concatenate(
        [jnp.broadcast_to(b[c * CHUNK + CHUNK - 1:(c + 1) * CHUNK], (CHUNK, D_QK))
         for c in range(nchunk)], axis=0)
    k_all = p_ref[slot, :, C_QK + D_QK:C_GATE]
    k_d = k_all * jnp.exp(blast_rows - b)
    v_all = pv_ref[slot]
    tok_chunk = _iota((TILE, 2 * DK), 0) >> 6
    incr = []
    for pair in range(HEADS // 2):
        kp = k_d[:, pair * 2 * DK:(pair + 1) * 2 * DK]
        lhs_t = jnp.concatenate(
            [jnp.where(tok_chunk == c, kp, 0.0) for c in range(nchunk)], axis=1).astype(BF16)
        incr.append(_dot_tn(lhs_t, v_all[:, pair * 2 * DV:(pair + 1) * 2 * DV]))

    ar = _iota((HEADS * CHUNK, CHUNK), 0) & (CHUNK - 1)
    ac = _iota((HEADS * CHUNK, CHUNK), 1)
    causal = ac <= ar

    state = list(state)
    o_chunks = []
    for c in range(nchunk):
        r0 = c * CHUNK
        bc = b[r0:r0 + CHUNK]
        qc = p_ref[slot, r0:r0 + CHUNK, C_QK:C_QK + D_QK] * Q_SCALE
        kc = p_ref[slot, r0:r0 + CHUNK, C_QK + D_QK:C_GATE]
        vc = pv_ref[slot, r0:r0 + CHUNK, :]
        bmid = bc[CHUNK // 2:CHUNK // 2 + 1]
        blast = bc[CHUNK - 1:CHUNK]
        q_in = (qc * jnp.exp(bc)).astype(BF16)
        q_a = qc * jnp.exp(bc - bmid)
        k_a = (kc * jnp.exp(bmid - bc)).astype(BF16)

        a_all = _dot_nt(_head_stack(q_a).astype(BF16), k_a)
        a_all = jnp.where(causal, a_all, 0.0).astype(BF16)
        for hd in range(HEADS):
            sbd_ref[c, hd * DK:(hd + 1) * DK, hd * DV:(hd + 1) * DV] = state[hd].astype(BF16)
        inter = _dot(q_in, sbd_ref[c])
        intra = jnp.concatenate(
            [_dot(a_all[hd * CHUNK:(hd + 1) * CHUNK], vc[:, hd * DV:(hd + 1) * DV])
             for hd in range(HEADS)], axis=1)
        o_chunks.append(inter + intra)

        dec = jnp.transpose(jnp.broadcast_to(jnp.exp(blast), (DV, D_QK)))
        for hd in range(HEADS):
            rows = slice(hd * DK, (hd + 1) * DK)
            pair, sub = divmod(hd, 2)
            i0 = c * 2 * DK + sub * DK
            state[hd] = state[hd] * dec[rows] + incr[pair][i0:i0 + DK, sub * DV:(sub + 1) * DV]
        emit()

    o = jnp.concatenate(o_chunks, axis=0)
    mix = []
    for hd in range(HEADS):
        cols = slice(hd * DV, (hd + 1) * DV)
        gate = p_ref[slot, :, C_GATE + hd * DV:C_GATE + (hd + 1) * DV]
        mix.append((_rmsnorm(o[:, cols], gng_ref[...]) * _silu(gate)).astype(BF16))
        emit()

    cw = cw_ref[...]
    row = _iota((TILE, 256), 0)
    new_tail = []
    for half in range(D_CONV // 256):
        cols = slice(half * 256, (half + 1) * 256)

        def conv_in(k, cols=cols):
            return p_ref[slot, :, C_CONV + k * D_CONV + cols.start:C_CONV + k * D_CONV + cols.stop]

        hc = conv_in(2) * conv_in(0)
        prev = tail[:, cols]
        h1 = jnp.where(row == 0, prev[7:8], pltpu.roll(hc, 1, 0))
        h2 = jnp.where(row == 0, prev[6:7], jnp.where(row == 1, prev[7:8], pltpu.roll(hc, 2, 0)))
        yc = cw[0:1, cols] * h2 + cw[1:2, cols] * h1 + cw[2:3, cols] * hc
        mix.append((conv_in(1) * yc * _silu(conv_in(3))).astype(BF16))
        new_tail.append(hc[TILE - 8:TILE])
        cout_ref[:, cols] = hc[TILE - 2:TILE]
        emit()

    out = p_ref[slot, :, C_X:C_QK] + _dot(jnp.concatenate(mix, axis=1), wo_ref[...])
    emit()
    half_rows = TILE // 2
    y_ref[0:half_rows] = _rmsnorm(out[0:half_rows], fg_ref[...])
    emit()
    y_ref[half_rows:TILE] = _rmsnorm(out[half_rows:TILE], fg_ref[...])
    return state, jnp.concatenate(new_tail, axis=1)


def _prompt_kernel(pairs_per_seq, xa_ref, xb_ref, ng_ref, wt_ref, wup_ref, bgk_ref, gng_ref, cw_ref,
                   wo_ref, fg_ref,
                   y_ref, sout_ref, cout_ref,
                   p_ref, pv_ref, s_ref, sbd_ref, tail_ref):
    step = pl.program_id(0)
    nchunk = TILE // CHUNK

    @pl.when(step == 0)
    def _():
        p_ref[0] = jnp.zeros(p_ref.shape[1:], F32)
        pv_ref[0] = jnp.zeros(pv_ref.shape[1:], BF16)
        sbd_ref[...] = jnp.zeros_like(sbd_ref)

    @pl.when(jnp.logical_or(step == 0, (step - 1) % pairs_per_seq == 0))
    def _():
        s_ref[...] = jnp.zeros_like(s_ref)
        tail_ref[...] = jnp.zeros_like(tail_ref)

    def half(x_ref, write_slot, read_slot, y_rows, sbd_slabs, state, tail):
        items = _project_items(x_ref, ng_ref, wt_ref, p_ref, pv_ref, write_slot)
        counts = iter(ITEM_SCHEDULE)
        items[0]()
        pending = iter(items[1:])

        def emit():
            for _ in range(next(counts)):
                next(pending)()

        out = _finish_tile(p_ref, pv_ref, read_slot, wup_ref, bgk_ref, gng_ref, cw_ref, wo_ref, fg_ref,
                           y_ref.at[y_rows], cout_ref, sbd_ref.at[sbd_slabs], state, tail, emit)
        assert next(counts, None) is None and next(pending, None) is None
        return out

    state = [s_ref[hd] for hd in range(HEADS)]
    tail = tail_ref[...]
    state, tail = half(xa_ref, 1, 0, pl.ds(0, TILE), pl.ds(0, nchunk), state, tail)
    state, tail = half(xb_ref, 0, 1, pl.ds(TILE, TILE), pl.ds(nchunk, nchunk), state, tail)
    for hd in range(HEADS):
        s_ref[hd] = state[hd]
        sout_ref[hd] = state[hd]
    tail_ref[...] = tail


def _sample_kernel(x_ref, cprev_ref, sin_ref, ng_ref, w_ref, wup_ref, bgk_ref,
                   gng_ref, cw_ref, wo_ref, fg_ref,
                   y_ref, sout_ref, cout_ref):
    nt = SEQ_BLK * DEC_LEN
    gt = GRP * DEC_LEN
    n_grp = SEQ_BLK // GRP

    def seqs(g):
        return slice(g * GRP, (g + 1) * GRP)

    x = jnp.concatenate([x_ref[seqs(g), t, :] for g in range(n_grp) for t in range(DEC_LEN)], axis=0)
    h = _rmsnorm(x, ng_ref[...]).astype(BF16)

    qk = _proj(h, w_ref, R_QK, R_V)
    v = _proj(h, w_ref, R_V, R_GATE).astype(BF16)
    gate = _proj(h, w_ref, R_GATE, R_CONV)
    g = _gate_log_decay(_proj(h, w_ref, R_LR, R_LR + RANK), wup_ref, bgk_ref)

    def tok(i):
        return (i >> 4) & (DEC_LEN - 1)

    def same_seq(r, c):
        return ((r >> 6) == (c >> 6)) & ((r & (GRP - 1)) == (c & (GRP - 1)))

    rt = _iota((nt, nt), 0)
    ct = _iota((nt, nt), 1)
    same = same_seq(rt, ct)
    cmask = jnp.where(same & (tok(ct) <= tok(rt)), 1.0, 0.0).astype(BF16)
    fmask = jnp.where(same, 1.0, 0.0).astype(BF16)
    b = _masked_sum(cmask, g)
    bl = _masked_sum(fmask, g)

    q = qk[:, 0:D_QK] * Q_SCALE
    k = qk[:, D_QK:2 * D_QK]
    q_in = (q * jnp.exp(b)).astype(BF16)
    k_a = (k * jnp.exp(-b)).astype(BF16)
    k_d = k * jnp.exp(bl - b)
    k_dt = jnp.transpose(k_d).astype(BF16)
    dec_t = jnp.transpose(jnp.exp(bl))

    ar = _iota((HEADS * gt, gt), 0) & (gt - 1)
    ac = _iota((HEADS * gt, gt), 1)
    amask = same_seq(ar, ac) & (tok(ac) <= tok(ar))
    ir = _iota((HEADS * gt, GRP * D_QK), 0)
    ic = _iota((HEADS * gt, GRP * D_QK), 1)
    imask = ((ir & (GRP - 1)) == (ic >> 8)) & ((ir >> 6) == ((ic >> 6) & 3))

    o_groups = []
    for gi in range(n_grp):
        r0 = gi * gt
        qg = q_in[r0:r0 + gt]
        a_all = _dot_nt(_head_stack(qg), k_a[r0:r0 + gt])
        a_all = jnp.where(amask, a_all, 0.0).astype(BF16)
        s_cat = sin_ref[seqs(gi)].reshape(GRP * D_QK, DV).astype(BF16)
        lhs = jnp.where(imask, jnp.tile(qg, (HEADS, GRP)), 0.0)
        inter = _dot(lhs, s_cat)
        outs = []
        for hd in range(HEADS):
            rows = slice(hd * gt, (hd + 1) * gt)
            intra = _dot(a_all[rows], v[r0:r0 + gt, hd * DV:(hd + 1) * DV])
            outs.append(inter[rows] + intra)
        o_groups.append(jnp.concatenate(outs, axis=1))
    o = _gla_epilogue(jnp.concatenate(o_groups, axis=0), gate, gng_ref)

    ur = _iota((SEQ_BLK * DK, nt), 0) >> 6
    uc = _iota((SEQ_BLK * DK, nt), 1)
    umask = ur == (((uc >> 6) << 4) | (uc & (GRP - 1)))
    for hd in range(HEADS):
        kt = jnp.tile(k_dt[hd * DK:(hd + 1) * DK], (SEQ_BLK, 1))
        u_h = _dot(jnp.where(umask, kt, 0.0), v[:, hd * DV:(hd + 1) * DV])
        for j in range(SEQ_BLK):
            col = (j // GRP) * gt + j % GRP
            dec = jnp.broadcast_to(dec_t[hd * DK:(hd + 1) * DK, col:col + 1], (DK, DV))
            sout_ref[j, hd] = sin_ref[j, hd] * dec + u_h[j * DK:(j + 1) * DK]

    conv = _proj(h, w_ref, R_CONV, R_LR)
    hc = conv[:, 2 * D_CONV:3 * D_CONV] * conv[:, 0:D_CONV]
    h1, h2 = [], []
    for gi in range(n_grp):
        r0 = gi * gt
        c0 = cprev_ref[seqs(gi), 0, :]
        c1 = cprev_ref[seqs(gi), 1, :]
        h1 += [c1, hc[r0:r0 + gt - GRP]]
        h2 += [c0, c1, hc[r0:r0 + gt - 2 * GRP]]
        cout_ref[seqs(gi), 0, :] = hc[r0 + gt - 2 * GRP:r0 + gt - GRP]
        cout_ref[seqs(gi), 1, :] = hc[r0 + gt - GRP:r0 + gt]
    cw = cw_ref[...]
    yc = cw[0:1] * jnp.concatenate(h2, axis=0) + cw[1:2] * jnp.concatenate(h1, axis=0) + cw[2:3] * hc
    yc = conv[:, D_CONV:2 * D_CONV] * yc * _silu(conv[:, 3 * D_CONV:4 * D_CONV])

    y = _out_proj(x, o, yc, wo_ref, fg_ref)
    for gi in range(n_grp):
        for t in range(DEC_LEN):
            r0 = gi * gt + t * GRP
            y_ref[seqs(gi), t, :] = y[r0:r0 + GRP]


def _const_spec(shape):
    return pl.BlockSpec(shape, lambda *_: (0,) * len(shape))


def _weight_specs():
    return [
        _const_spec((1, D_MODEL)),
        _const_spec((D_MODEL, N_W)),
        _const_spec((RANK, D_QK)),
        _const_spec((1, D_QK)),
        _const_spec((1, DV)),
        _const_spec((3, D_CONV)),
        _const_spec((D_MODEL, D_MODEL)),
        _const_spec((1, D_MODEL)),
    ]


def _prep_kernel(wt_ref, wo_ref, w_ref, wob_ref):
    w_ref[...] = jnp.transpose(wt_ref[...]).astype(BF16)
    wob_ref[...] = wo_ref[...].astype(BF16)


def _prepare_weights(w_in, w_out):
    n_qkvg = SRC_LR // PREP_BLK
    n_conv = (4 * D_CONV) // PREP_BLK
    n_blk = N_W // PREP_BLK
    assert n_blk == n_qkvg + n_conv + 1

    def src_row(i):
        row = jnp.where(i < n_qkvg, PREP_BLK * i,
                        jnp.where(i < n_qkvg + n_conv, SRC_CONV + PREP_BLK * (i - n_qkvg), SRC_LR))
        return pl.multiple_of(row, RANK)

    wo_rows = D_MODEL // n_blk
    return pl.pallas_call(
        _prep_kernel,
        grid=(n_blk,),
        in_specs=[
            pl.BlockSpec((pl.Element(PREP_BLK), pl.Element(D_MODEL)), lambda i: (src_row(i), 0)),
            pl.BlockSpec((wo_rows, D_MODEL), lambda i: (i, 0)),
        ],
        out_specs=[
            pl.BlockSpec((D_MODEL, PREP_BLK), lambda i: (0, i)),
            pl.BlockSpec((wo_rows, D_MODEL), lambda i: (i, 0)),
        ],
        out_shape=[
            jax.ShapeDtypeStruct((D_MODEL, N_W), BF16),
            jax.ShapeDtypeStruct((D_MODEL, D_MODEL), BF16),
        ],
        compiler_params=pltpu.CompilerParams(dimension_semantics=("arbitrary",)),
        name="weight_prep",
    )(jnp.swapaxes(w_in[0], 0, 1), w_out[0])


def kernel(x_prompt, x_sample, state_gla, state_conv, norm_gain, w_in, w_gk_up, b_gk,
           gla_norm_gain, conv_w, w_out, final_norm_gain):
    n_batch, seq_len, _ = x_prompt.shape
    n_dec = x_sample.shape[0]
    tiles_per_seq = seq_len // TILE
    n_tiles = n_batch * tiles_per_seq
    w_proj, w_o = _prepare_weights(w_in, w_out)
    weights = (
        norm_gain.reshape(1, D_MODEL),
        w_proj,
        w_gk_up[0].astype(BF16),
        b_gk.reshape(1, D_QK),
        gla_norm_gain.reshape(1, DV),
        conv_w[0],
        w_o,
        final_norm_gain.reshape(1, D_MODEL),
    )

    pairs_per_seq = tiles_per_seq // 2

    def tile_index(t):
        t = jnp.clip(t, 0, n_tiles - 1)
        return (t // tiles_per_seq, t % tiles_per_seq, 0)

    def out_pair(j):
        p = jnp.maximum(j - 1, 0)
        return (p // pairs_per_seq, p % pairs_per_seq, 0)

    def out_seq(j):
        return jnp.maximum(j - 1, 0) // pairs_per_seq

    y_p, s_p, c_p = pl.pallas_call(
        functools.partial(_prompt_kernel, pairs_per_seq),
        grid=(n_tiles // 2 + 1,),
        in_specs=[pl.BlockSpec((None, TILE, D_MODEL), lambda j: tile_index(2 * j - 1)),
                  pl.BlockSpec((None, TILE, D_MODEL), lambda j: tile_index(2 * j))] + _weight_specs(),
        out_specs=[
            pl.BlockSpec((None, 2 * TILE, D_MODEL), out_pair),
            pl.BlockSpec((None, None, HEADS, DK, DV), lambda j: (0, out_seq(j), 0, 0, 0)),
            pl.BlockSpec((None, None, 2, D_CONV), lambda j: (0, out_seq(j), 0, 0)),
        ],
        out_shape=[
            jax.ShapeDtypeStruct((n_batch, seq_len, D_MODEL), F32),
            jax.ShapeDtypeStruct((1, n_batch, HEADS, DK, DV), F32),
            jax.ShapeDtypeStruct((1, n_batch, 2, D_CONV), F32),
        ],
        scratch_shapes=[
            pltpu.VMEM((2, TILE, C_END), F32),
            pltpu.VMEM((2, TILE, D_GLA), BF16),
            pltpu.VMEM((HEADS, DK, DV), F32),
            pltpu.VMEM((2 * (TILE // CHUNK), D_QK, D_GLA), BF16),
            pltpu.VMEM((8, D_CONV), F32),
        ],
        compiler_params=pltpu.CompilerParams(
            dimension_semantics=("arbitrary",), vmem_limit_bytes=VMEM_LIMIT),
        name="gla_conv_prompt",
    )(x_prompt, x_prompt, *weights)

    y_s, s_s, c_s = pl.pallas_call(
        _sample_kernel,
        grid=(n_dec // SEQ_BLK,),
        in_specs=[
            pl.BlockSpec((SEQ_BLK, DEC_LEN, D_MODEL), lambda i: (i, 0, 0)),
            pl.BlockSpec((None, SEQ_BLK, 2, D_CONV), lambda i: (0, i, 0, 0)),
            pl.BlockSpec((None, SEQ_BLK, HEADS, DK, DV), lambda i: (0, i, 0, 0, 0)),
        ] + _weight_specs(),
        out_specs=[
            pl.BlockSpec((SEQ_BLK, DEC_LEN, D_MODEL), lambda i: (i, 0, 0)),
            pl.BlockSpec((None, SEQ_BLK, HEADS, DK, DV), lambda i: (0, i, 0, 0, 0)),
            pl.BlockSpec((None, SEQ_BLK, 2, D_CONV), lambda i: (0, i, 0, 0)),
        ],
        out_shape=[
            jax.ShapeDtypeStruct((n_dec, DEC_LEN, D_MODEL), F32),
            jax.ShapeDtypeStruct((1, n_dec, HEADS, DK, DV), F32),
            jax.ShapeDtypeStruct((1, n_dec, 2, D_CONV), F32),
        ],
        compiler_params=pltpu.CompilerParams(
            dimension_semantics=("arbitrary",), vmem_limit_bytes=VMEM_LIMIT),
        name="gla_conv_sample",
    )(x_sample, state_conv, state_gla, *weights)

    return (y_p, y_s, s_p, c_p, s_s, c_s)
```

```python
import functools

import jax
import jax.numpy as jnp
from jax import lax
from jax.experimental import pallas as pl
from jax.experimental.pallas import tpu as pltpu

D_MODEL = 1024
HEADS = 4
DK = 64
DV = 128
D_QK = HEADS * DK
D_GLA = HEADS * DV
D_CONV = 512
RANK = 16
CHUNK = 64
TILE = 256
DEC_LEN = 4
SEQ_BLK = 32
GRP = 16
EPS = 1e-6
Q_SCALE = DK ** -0.5
GATE_SCALE = 1.0 / 16.0
VMEM_LIMIT = 52 * 1024 * 1024

R_QK = 0
R_V = 2 * D_QK
R_GATE = R_V + D_GLA
R_CONV = R_GATE + D_GLA
R_LR = R_CONV + 4 * D_CONV
PREP_BLK = 512
N_W = R_LR + PREP_BLK
SRC_LR = 2 * D_QK + 2 * D_GLA
SRC_CONV = SRC_LR + RANK

C_X = 0
C_QK = C_X + D_MODEL
C_GATE = C_QK + 2 * D_QK
C_CONV = C_GATE + D_GLA
C_LR = C_CONV + 4 * D_CONV
C_END = C_LR + 128

F32 = jnp.float32
BF16 = jnp.bfloat16


def _dot(a, b):
    return jnp.dot(a, b, preferred_element_type=F32)


def _dot_nt(a, b):
    return lax.dot_general(a, b, (((1,), (1,)), ((), ())), preferred_element_type=F32)


def _proj(h, w_ref, lo, hi):
    return _dot(h, w_ref[:, lo:hi])


def _dot_tn(a, b):
    return lax.dot_general(a, b, (((0,), (0,)), ((), ())), preferred_element_type=F32)


def _rmsnorm(x, gain):
    ms = jnp.mean(x * x, axis=-1, keepdims=True)
    return x * lax.rsqrt(ms + EPS) * gain


def _silu(x):
    return x * (1.0 / (1.0 + jnp.exp(-x)))


def _log_sigmoid(z):
    return jnp.minimum(z, 0.0) - jnp.log(1.0 + jnp.exp(-jnp.abs(z)))


def _iota(shape, dim):
    return lax.broadcasted_iota(jnp.int32, shape, dim)


def _masked_sum(mask_bf16, g):
    g1 = g.astype(BF16)
    r1 = g - g1.astype(F32)
    g2 = r1.astype(BF16)
    g3 = (r1 - g2.astype(F32)).astype(BF16)
    return _dot(mask_bf16, g1) + _dot(mask_bf16, g2) + _dot(mask_bf16, g3)


def _head_stack(q):
    lane_head = _iota(q.shape, 1) >> 6
    return jnp.concatenate([jnp.where(lane_head == h, q, 0.0) for h in range(HEADS)], axis=0)


def _gate_log_decay(lr, wup_ref, bgk_ref):
    z = _dot(lr.astype(BF16), wup_ref[...]) + bgk_ref[...]
    return _log_sigmoid(z) * GATE_SCALE


def _gla_epilogue(o, gate, gng_ref):
    outs = []
    for hd in range(HEADS):
        oh = o[:, hd * DV:(hd + 1) * DV]
        outs.append(_rmsnorm(oh, gng_ref[...]))
    return jnp.concatenate(outs, axis=1) * _silu(gate)


def _out_proj(x, o, yc, wo_ref, fg_ref):
    mix = jnp.concatenate([o, yc], axis=1).astype(BF16)
    out = x + _dot(mix, wo_ref[...])
    return _rmsnorm(out, fg_ref[...])


ITEM_SCHEDULE = (0, 4, 0, 1, 0, 0, 0, 0, 0, 0, 0, 6, 3, 1)
ITEM_SCHEDULE_B = (0, 3, 0, 1, 1, 1, 0, 0, 0, 0, 0, 5, 3, 1)


def _project_items(x_ref, ng_ref, wt_ref, p_ref, pv_ref, slot):
    cache = {}

    def norm():
        x = x_ref[...]
        cache["h"] = _rmsnorm(x, ng_ref[...]).astype(BF16)
        p_ref[slot, :, C_X:C_QK] = x

    def to_p(row, col, width=256):
        def item():
            p_ref[slot, :, col:col + width] = _proj(cache["h"], wt_ref, row, row + width)
        return item

    def to_pv(off):
        def item():
            pv_ref[slot, :, off:off + 256] = _proj(
                cache["h"], wt_ref, R_V + off, R_V + off + 256).astype(BF16)
        return item

    items = [norm, to_p(R_LR, C_LR, RANK)]
    items += [to_p(R_QK + o, C_QK + o) for o in (0, 256)]
    items += [to_pv(o) for o in (0, 256)]
    items += [to_p(R_CONV + o, C_CONV + o) for o in range(0, 4 * D_CONV, 256)]
    items += [to_p(R_GATE + o, C_GATE + o) for o in (0, 256)]
    return items


def _finish_tile(p_ref, pv_ref, slot, wup_ref, bgk_ref, gng_ref, cw_ref, wo_ref, fg_ref,
                 y_ref, cout_ref, sbd_ref, state, tail, emit):
    g = _gate_log_decay(p_ref[slot, :, C_LR:C_LR + RANK], wup_ref, bgk_ref)
    emit()

    rt = _iota((TILE, TILE), 0)
    ct = _iota((TILE, TILE), 1)
    cmask = jnp.where(((rt >> 6) == (ct >> 6)) & (ct <= rt), 1.0, 0.0).astype(BF16)
    b = _masked_sum(cmask, g)
    emit()

    nchunk = TILE // CHUNK
    blast_rows = jnp.concatenate(
        [jnp.broadcast_to(b[c * CHUNK + CHUNK - 1:(c + 1) * CHUNK], (CHUNK, D_QK))
         for c in range(nchunk)], axis=0)
    k_all = p_ref[slot, :, C_QK + D_QK:C_GATE]
    k_d = k_all * jnp.exp(blast_rows - b)
    v_all = pv_ref[slot]
    tok_chunk = _iota((TILE, 2 * DK), 0) >> 6
    incr = []
    for pair in range(HEADS // 2):
        kp = k_d[:, pair * 2 * DK:(pair + 1) * 2 * DK]
        lhs_t = jnp.concatenate(
            [jnp.where(tok_chunk == c, kp, 0.0) for c in range(nchunk)], axis=1).astype(BF16)
        incr.append(_dot_tn(lhs_t, v_all[:, pair * 2 * DV:(pair + 1) * 2 * DV]))

    ar = _iota((HEADS * CHUNK, CHUNK), 0) & (CHUNK - 1)
    ac = _iota((HEADS * CHUNK, CHUNK), 1)
    causal = ac <= ar

    state = list(state)
    o_chunks = []
    for c in range(nchunk):
        r0 = c * CHUNK
        bc = b[r0:r0 + CHUNK]
        qc = p_ref[slot, r0:r0 + CHUNK, C_QK:C_QK + D_QK] * Q_SCALE
        kc = p_ref[slot, r0:r0 + CHUNK, C_QK + D_QK:C_GATE]
        vc = pv_ref[slot, r0:r0 + CHUNK, :]
        bmid = bc[CHUNK // 2:CHUNK // 2 + 1]
        blast = bc[CHUNK - 1:CHUNK]
        q_in = (qc * jnp.exp(bc)).astype(BF16)
        q_a = qc * jnp.exp(bc - bmid)
        k_a = (kc * jnp.exp(bmid - bc)).astype(BF16)

        a_all = _dot_nt(_head_stack(q_a).astype(BF16), k_a)
        a_all = jnp.where(causal, a_all, 0.0).astype(BF16)
        for hd in range(HEADS):
            sbd_ref[c, hd * DK:(hd + 1) * DK, hd * DV:(hd + 1) * DV] = state[hd].astype(BF16)
        inter = _dot(q_in, sbd_ref[c])
        intra = jnp.concatenate(
            [_dot(a_all[hd * CHUNK:(hd + 1) * CHUNK], vc[:, hd * DV:(hd + 1) * DV])
             for hd in range(HEADS)], axis=1)
        o_chunks.append(inter + intra)

        dec = jnp.transpose(jnp.broadcast_to(jnp.exp(blast), (DV, D_QK)))
        for hd in range(HEADS):
            rows = slice(hd * DK, (hd + 1) * DK)
            pair, sub = divmod(hd, 2)
            i0 = c * 2 * DK + sub * DK
            state[hd] = state[hd] * dec[rows] + incr[pair][i0:i0 + DK, sub * DV:(sub + 1) * DV]
        emit()

    o = jnp.concatenate(o_chunks, axis=0)
    mix = []
    for hd in range(HEADS):
        cols = slice(hd * DV, (hd + 1) * DV)
        gate = p_ref[slot, :, C_GATE + hd * DV:C_GATE + (hd + 1) * DV]
        mix.append((_rmsnorm(o[:, cols], gng_ref[...]) * _silu(gate)).astype(BF16))
        emit()

    cw = cw_ref[...]
    row = _iota((TILE, 256), 0)
    new_tail = []
    for half in range(D_CONV // 256):
        cols = slice(half * 256, (half + 1) * 256)

        def conv_in(k, cols=cols):
            return p_ref[slot, :, C_CONV + k * D_CONV + cols.start:C_CONV + k * D_CONV + cols.stop]

        hc = conv_in(2) * conv_in(0)
        prev = tail[:, cols]
        h1 = jnp.where(row == 0, prev[7:8], pltpu.roll(hc, 1, 0))
        h2 = jnp.where(row == 0, prev[6:7], jnp.where(row == 1, prev[7:8], pltpu.roll(hc, 2, 0)))
        yc = cw[0:1, cols] * h2 + cw[1:2, cols] * h1 + cw[2:3, cols] * hc
        mix.append((conv_in(1) * yc * _silu(conv_in(3))).astype(BF16))
        new_tail.append(hc[TILE - 8:TILE])
        cout_ref[:, cols] = hc[TILE - 2:TILE]
        emit()

    out = p_ref[slot, :, C_X:C_QK] + _dot(jnp.concatenate(mix, axis=1), wo_ref[...])
    emit()
    half_rows = TILE // 2
    y_ref[0:half_rows] = _rmsnorm(out[0:half_rows], fg_ref[...])
    emit()
    y_ref[half_rows:TILE] = _rmsnorm(out[half_rows:TILE], fg_ref[...])
    return state, jnp.concatenate(new_tail, axis=1)


def _prompt_kernel(pairs_per_seq, xa_ref, xb_ref, ng_ref, wt_ref, wup_ref, bgk_ref, gng_ref, cw_ref,
                   wo_ref, fg_ref,
                   y_ref, sout_ref, cout_ref,
                   p_ref, pv_ref, s_ref, sbd_ref, tail_ref):
    step = pl.program_id(0)
    nchunk = TILE // CHUNK

    @pl.when(step == 0)
    def _():
        p_ref[0] = jnp.zeros(p_ref.shape[1:], F32)
        pv_ref[0] = jnp.zeros(pv_ref.shape[1:], BF16)
        sbd_ref[...] = jnp.zeros_like(sbd_ref)

    @pl.when(jnp.logical_or(step == 0, (step - 1) % pairs_per_seq == 0))
    def _():
        s_ref[...] = jnp.zeros_like(s_ref)
        tail_ref[...] = jnp.zeros_like(tail_ref)

    def half(x_ref, write_slot, read_slot, y_rows, sbd_slabs, state, tail, schedule):
        items = _project_items(x_ref, ng_ref, wt_ref, p_ref, pv_ref, write_slot)
        counts = iter(schedule)
        items[0]()
        pending = iter(items[1:])

        def emit():
            for _ in range(next(counts)):
                next(pending)()

        out = _finish_tile(p_ref, pv_ref, read_slot, wup_ref, bgk_ref, gng_ref, cw_ref, wo_ref, fg_ref,
                           y_ref.at[y_rows], cout_ref, sbd_ref.at[sbd_slabs], state, tail, emit)
        assert next(counts, None) is None and next(pending, None) is None
        return out

    state = [s_ref[hd] for hd in range(HEADS)]
    tail = tail_ref[...]
    state, tail = half(xa_ref, 1, 0, pl.ds(0, TILE), pl.ds(0, nchunk), state, tail, ITEM_SCHEDULE)
    state, tail = half(xb_ref, 0, 1, pl.ds(TILE, TILE), pl.ds(nchunk, nchunk), state, tail, ITEM_SCHEDULE_B)
    for hd in range(HEADS):
        s_ref[hd] = state[hd]
        sout_ref[hd] = state[hd]
    tail_ref[...] = tail


def _sample_kernel(x_ref, cprev_ref, sin_ref, ng_ref, w_ref, wup_ref, bgk_ref,
                   gng_ref, cw_ref, wo_ref, fg_ref,
                   y_ref, sout_ref, cout_ref):
    nt = SEQ_BLK * DEC_LEN
    gt = GRP * DEC_LEN
    n_grp = SEQ_BLK // GRP

    def seqs(g):
        return slice(g * GRP, (g + 1) * GRP)

    x = jnp.concatenate([x_ref[seqs(g), t, :] for g in range(n_grp) for t in range(DEC_LEN)], axis=0)
    h = _rmsnorm(x, ng_ref[...]).astype(BF16)

    qk = _proj(h, w_ref, R_QK, R_V)
    v = _proj(h, w_ref, R_V, R_GATE).astype(BF16)
    gate = _proj(h, w_ref, R_GATE, R_CONV)
    g = _gate_log_decay(_proj(h, w_ref, R_LR, R_LR + RANK), wup_ref, bgk_ref)

    def tok(i):
        return (i >> 4) & (DEC_LEN - 1)

    def same_seq(r, c):
        return ((r >> 6) == (c >> 6)) & ((r & (GRP - 1)) == (c & (GRP - 1)))

    rt = _iota((nt, nt), 0)
    ct = _iota((nt, nt), 1)
    same = same_seq(rt, ct)
    cmask = jnp.where(same & (tok(ct) <= tok(rt)), 1.0, 0.0).astype(BF16)
    fmask = jnp.where(same, 1.0, 0.0).astype(BF16)
    b = _masked_sum(cmask, g)
    bl = _masked_sum(fmask, g)

    q = qk[:, 0:D_QK] * Q_SCALE
    k = qk[:, D_QK:2 * D_QK]
    q_in = (q * jnp.exp(b)).astype(BF16)
    k_a = (k * jnp.exp(-b)).astype(BF16)
    k_d = k * jnp.exp(bl - b)
    k_dt = jnp.transpose(k_d).astype(BF16)
    dec_t = jnp.transpose(jnp.exp(bl))

    ar = _iota((HEADS * gt, gt), 0) & (gt - 1)
    ac = _iota((HEADS * gt, gt), 1)
    amask = same_seq(ar, ac) & (tok(ac) <= tok(ar))
    ir = _iota((HEADS * gt, GRP * D_QK), 0)
    ic = _iota((HEADS * gt, GRP * D_QK), 1)
    imask = ((ir & (GRP - 1)) == (ic >> 8)) & ((ir >> 6) == ((ic >> 6) & 3))

    o_groups = []
    for gi in range(n_grp):
        r0 = gi * gt
        qg = q_in[r0:r0 + gt]
        a_all = _dot_nt(_head_stack(qg), k_a[r0:r0 + gt])
        a_all = jnp.where(amask, a_all, 0.0).astype(BF16)
        s_cat = sin_ref[seqs(gi)].reshape(GRP * D_QK, DV).astype(BF16)
        lhs = jnp.where(imask, jnp.tile(qg, (HEADS, GRP)), 0.0)
        inter = _dot(lhs, s_cat)
        outs = []
        for hd in range(HEADS):
            rows = slice(hd * gt, (hd + 1) * gt)
            intra = _dot(a_all[rows], v[r0:r0 + gt, hd * DV:(hd + 1) * DV])
            outs.append(inter[rows] + intra)
        o_groups.append(jnp.concatenate(outs, axis=1))
    o = _gla_epilogue(jnp.concatenate(o_groups, axis=0), gate, gng_ref)

    ur = _iota((SEQ_BLK * DK, nt), 0) >> 6
    uc = _iota((SEQ_BLK * DK, nt), 1)
    umask = ur == (((uc >> 6) << 4) | (uc & (GRP - 1)))
    for hd in range(HEADS):
        kt = jnp.tile(k_dt[hd * DK:(hd + 1) * DK], (SEQ_BLK, 1))
        u_h = _dot(jnp.where(umask, kt, 0.0), v[:, hd * DV:(hd + 1) * DV])
        for j in range(SEQ_BLK):
            col = (j // GRP) * gt + j % GRP
            dec = jnp.broadcast_to(dec_t[hd * DK:(hd + 1) * DK, col:col + 1], (DK, DV))
            sout_ref[j, hd] = sin_ref[j, hd] * dec + u_h[j * DK:(j + 1) * DK]

    conv = _proj(h, w_ref, R_CONV, R_LR)
    hc = conv[:, 2 * D_CONV:3 * D_CONV] * conv[:, 0:D_CONV]
    h1, h2 = [], []
    for gi in range(n_grp):
        r0 = gi * gt
        c0 = cprev_ref[seqs(gi), 0, :]
        c1 = cprev_ref[seqs(gi), 1, :]
        h1 += [c1, hc[r0:r0 + gt - GRP]]
        h2 += [c0, c1, hc[r0:r0 + gt - 2 * GRP]]
        cout_ref[seqs(gi), 0, :] = hc[r0 + gt - 2 * GRP:r0 + gt - GRP]
        cout_ref[seqs(gi), 1, :] = hc[r0 + gt - GRP:r0 + gt]
    cw = cw_ref[...]
    yc = cw[0:1] * jnp.concatenate(h2, axis=0) + cw[1:2] * jnp.concatenate(h1, axis=0) + cw[2:3] * hc
    yc = conv[:, D_CONV:2 * D_CONV] * yc * _silu(conv[:, 3 * D_CONV:4 * D_CONV])

    y = _out_proj(x, o, yc, wo_ref, fg_ref)
    for gi in range(n_grp):
        for t in range(DEC_LEN):
            r0 = gi * gt + t * GRP
            y_ref[seqs(gi), t, :] = y[r0:r0 + GRP]


def _const_spec(shape):
    return pl.BlockSpec(shape, lambda *_: (0,) * len(shape))


def _weight_specs():
    return [
        _const_spec((1, D_MODEL)),
        _const_spec((D_MODEL, N_W)),
        _const_spec((RANK, D_QK)),
        _const_spec((1, D_QK)),
        _const_spec((1, DV)),
        _const_spec((3, D_CONV)),
        _const_spec((D_MODEL, D_MODEL)),
        _const_spec((1, D_MODEL)),
    ]


def _prep_kernel(wt_ref, wo_ref, w_ref, wob_ref):
    w_ref[...] = jnp.transpose(wt_ref[...]).astype(BF16)
    wob_ref[...] = wo_ref[...].astype(BF16)


def _prepare_weights(w_in, w_out):
    n_qkvg = SRC_LR // PREP_BLK
    n_conv = (4 * D_CONV) // PREP_BLK
    n_blk = N_W // PREP_BLK
    assert n_blk == n_qkvg + n_conv + 1

    def src_row(i):
        row = jnp.where(i < n_qkvg, PREP_BLK * i,
                        jnp.where(i < n_qkvg + n_conv, SRC_CONV + PREP_BLK * (i - n_qkvg), SRC_LR))
        return pl.multiple_of(row, RANK)

    wo_rows = D_MODEL // n_blk
    return pl.pallas_call(
        _prep_kernel,
        grid=(n_blk,),
        in_specs=[
            pl.BlockSpec((pl.Element(PREP_BLK), pl.Element(D_MODEL)), lambda i: (src_row(i), 0)),
            pl.BlockSpec((wo_rows, D_MODEL), lambda i: (i, 0)),
        ],
        out_specs=[
            pl.BlockSpec((D_MODEL, PREP_BLK), lambda i: (0, i)),
            pl.BlockSpec((wo_rows, D_MODEL), lambda i: (i, 0)),
        ],
        out_shape=[
            jax.ShapeDtypeStruct((D_MODEL, N_W), BF16),
            jax.ShapeDtypeStruct((D_MODEL, D_MODEL), BF16),
        ],
        compiler_params=pltpu.CompilerParams(dimension_semantics=("arbitrary",)),
        name="weight_prep",
    )(jnp.swapaxes(w_in[0], 0, 1), w_out[0])


def kernel(x_prompt, x_sample, state_gla, state_conv, norm_gain, w_in, w_gk_up, b_gk,
           gla_norm_gain, conv_w, w_out, final_norm_gain):
    n_batch, seq_len, _ = x_prompt.shape
    n_dec = x_sample.shape[0]
    tiles_per_seq = seq_len // TILE
    n_tiles = n_batch * tiles_per_seq
    w_proj, w_o = _prepare_weights(w_in, w_out)
    weights = (
        norm_gain.reshape(1, D_MODEL),
        w_proj,
        w_gk_up[0].astype(BF16),
        b_gk.reshape(1, D_QK),
        gla_norm_gain.reshape(1, DV),
        conv_w[0],
        w_o,
        final_norm_gain.reshape(1, D_MODEL),
    )

    pairs_per_seq = tiles_per_seq // 2

    def tile_index(t):
        t = jnp.clip(t, 0, n_tiles - 1)
        return (t // tiles_per_seq, t % tiles_per_seq, 0)

    def out_pair(j):
        p = jnp.maximum(j - 1, 0)
        return (p // pairs_per_seq, p % pairs_per_seq, 0)

    def out_seq(j):
        return jnp.maximum(j - 1, 0) // pairs_per_seq

    y_p, s_p, c_p = pl.pallas_call(
        functools.partial(_prompt_kernel, pairs_per_seq),
        grid=(n_tiles // 2 + 1,),
        in_specs=[pl.BlockSpec((None, TILE, D_MODEL), lambda j: tile_index(2 * j - 1)),
                  pl.BlockSpec((None, TILE, D_MODEL), lambda j: tile_index(2 * j))] + _weight_specs(),
        out_specs=[
            pl.BlockSpec((None, 2 * TILE, D_MODEL), out_pair),
            pl.BlockSpec((None, None, HEADS, DK, DV), lambda j: (0, out_seq(j), 0, 0, 0)),
            pl.BlockSpec((None, None, 2, D_CONV), lambda j: (0, out_seq(j), 0, 0)),
        ],
        out_shape=[
            jax.ShapeDtypeStruct((n_batch, seq_len, D_MODEL), F32),
            jax.ShapeDtypeStruct((1, n_batch, HEADS, DK, DV), F32),
            jax.ShapeDtypeStruct((1, n_batch, 2, D_CONV), F32),
        ],
        scratch_shapes=[
            pltpu.VMEM((2, TILE, C_END), F32),
            pltpu.VMEM((2, TILE, D_GLA), BF16),
            pltpu.VMEM((HEADS, DK, DV), F32),
            pltpu.VMEM((2 * (TILE // CHUNK), D_QK, D_GLA), BF16),
            pltpu.VMEM((8, D_CONV), F32),
        ],
        compiler_params=pltpu.CompilerParams(
            dimension_semantics=("arbitrary",), vmem_limit_bytes=VMEM_LIMIT),
        name="gla_conv_prompt",
    )(x_prompt, x_prompt, *weights)

    y_s, s_s, c_s = pl.pallas_call(
        _sample_kernel,
        grid=(n_dec // SEQ_BLK,),
        in_specs=[
            pl.BlockSpec((SEQ_BLK, DEC_LEN, D_MODEL), lambda i: (i, 0, 0)),
            pl.BlockSpec((None, SEQ_BLK, 2, D_CONV), lambda i: (0, i, 0, 0)),
            pl.BlockSpec((None, SEQ_BLK, HEADS, DK, DV), lambda i: (0, i, 0, 0, 0)),
        ] + _weight_specs(),
        out_specs=[
            pl.BlockSpec((SEQ_BLK, DEC_LEN, D_MODEL), lambda i: (i, 0, 0)),
            pl.BlockSpec((None, SEQ_BLK, HEADS, DK, DV), lambda i: (0, i, 0, 0, 0)),
            pl.BlockSpec((None, SEQ_BLK, 2, D_CONV), lambda i: (0, i, 0, 0)),
        ],
        out_shape=[
            jax.ShapeDtypeStruct((n_dec, DEC_LEN, D_MODEL), F32),
            jax.ShapeDtypeStruct((1, n_dec, HEADS, DK, DV), F32),
            jax.ShapeDtypeStruct((1, n_dec, 2, D_CONV), F32),
        ],
        compiler_params=pltpu.CompilerParams(
            dimension_semantics=("arbitrary",), vmem_limit_bytes=VMEM_LIMIT),
        name="gla_conv_sample",
    )(x_sample, state_conv, state_gla, *weights)

    return (y_p, y_s, s_p, c_p, s_s, c_s)
```

```python
import functools

import jax
import jax.numpy as jnp
from jax import lax
from jax.experimental import pallas as pl
from jax.experimental.pallas import tpu as pltpu

D_MODEL = 1024
HEADS = 4
DK = 64
DV = 128
D_QK = HEADS * DK
D_GLA = HEADS * DV
D_CONV = 512
RANK = 16
CHUNK = 64
TILE = 256
DEC_LEN = 4
SEQ_BLK = 32
GRP = 16
EPS = 1e-6
Q_SCALE = DK ** -0.5
GATE_SCALE = 1.0 / 16.0
VMEM_LIMIT = 52 * 1024 * 1024

R_QK = 0
R_V = 2 * D_QK
R_GATE = R_V + D_GLA
R_CONV = R_GATE + D_GLA
R_LR = R_CONV + 4 * D_CONV
PREP_BLK = 512
N_W = R_LR + PREP_BLK
SRC_LR = 2 * D_QK + 2 * D_GLA
SRC_CONV = SRC_LR + RANK

C_X = 0
C_QK = C_X + D_MODEL
C_GATE = C_QK + 2 * D_QK
C_CONV = C_GATE + D_GLA
C_LR = C_CONV + 4 * D_CONV
C_END = C_LR + 128

F32 = jnp.float32
BF16 = jnp.bfloat16


def _dot(a, b):
    return jnp.dot(a, b, preferred_element_type=F32)


def _dot_nt(a, b):
    return lax.dot_general(a, b, (((1,), (1,)), ((), ())), preferred_element_type=F32)


def _proj(h, w_ref, lo, hi):
    return _dot(h, w_ref[:, lo:hi])


def _dot_tn(a, b):
    return lax.dot_general(a, b, (((0,), (0,)), ((), ())), preferred_element_type=F32)


def _rmsnorm(x, gain):
    ms = jnp.mean(x * x, axis=-1, keepdims=True)
    return x * lax.rsqrt(ms + EPS) * gain


def _silu(x):
    return x * (1.0 / (1.0 + jnp.exp(-x)))


def _log_sigmoid(z):
    return jnp.minimum(z, 0.0) - jnp.log(1.0 + jnp.exp(-jnp.abs(z)))


def _iota(shape, dim):
    return lax.broadcasted_iota(jnp.int32, shape, dim)


def _masked_sum(mask_bf16, g):
    g1 = g.astype(BF16)
    r1 = g - g1.astype(F32)
    g2 = r1.astype(BF16)
    g3 = (r1 - g2.astype(F32)).astype(BF16)
    return _dot(mask_bf16, g1) + _dot(mask_bf16, g2) + _dot(mask_bf16, g3)


def _head_stack(q):
    lane_head = _iota(q.shape, 1) >> 6
    return jnp.concatenate([jnp.where(lane_head == h, q, 0.0) for h in range(HEADS)], axis=0)


def _gate_log_decay(lr, wup_ref, bgk_ref):
    z = _dot(lr.astype(BF16), wup_ref[...]) + bgk_ref[...]
    return _log_sigmoid(z) * GATE_SCALE


def _gla_epilogue(o, gate, gng_ref):
    outs = []
    for hd in range(HEADS):
        oh = o[:, hd * DV:(hd + 1) * DV]
        outs.append(_rmsnorm(oh, gng_ref[...]))
    return jnp.concatenate(outs, axis=1) * _silu(gate)


def _out_proj(x, o, yc, wo_ref, fg_ref):
    mix = jnp.concatenate([o, yc], axis=1).astype(BF16)
    out = x + _dot(mix, wo_ref[...])
    return _rmsnorm(out, fg_ref[...])


ITEM_SCHEDULE = (0, 4, 0, 1, 0, 0, 0, 0, 0, 0, 0, 6, 3, 1)
ITEM_SCHEDULE_B = (0, 3, 0, 1, 1, 1, 0, 0, 0, 0, 0, 5, 3, 1)


def _project_items(x_ref, ng_ref, wt_ref, p_ref, pv_ref, slot):
    cache = {}

    def norm():
        x = x_ref[...]
        cache["h"] = _rmsnorm(x, ng_ref[...]).astype(BF16)
        p_ref[slot, :, C_X:C_QK] = x

    def to_p(row, col, width=256):
        def item():
            p_ref[slot, :, col:col + width] = _proj(cache["h"], wt_ref, row, row + width)
        return item

    def to_pv(off):
        def item():
            pv_ref[slot, :, off:off + 256] = _proj(
                cache["h"], wt_ref, R_V + off, R_V + off + 256).astype(BF16)
        return item

    items = [norm, to_p(R_LR, C_LR, RANK)]
    items += [to_p(R_QK + o, C_QK + o) for o in (0, 256)]
    items += [to_pv(o) for o in (0, 256)]
    items += [to_p(R_CONV + o, C_CONV + o) for o in range(0, 4 * D_CONV, 256)]
    items += [to_p(R_GATE + o, C_GATE + o) for o in (0, 256)]
    return items


def _finish_tile(p_ref, pv_ref, slot, wup_ref, bgk_ref, gng_ref, cw_ref, wo_ref, fg_ref,
                 y_ref, cout_ref, sbd_ref, state, tail, emit):
    g = _gate_log_decay(p_ref[slot, :, C_LR:C_LR + RANK], wup_ref, bgk_ref)
    emit()

    rt = _iota((TILE, TILE), 0)
    ct = _iota((TILE, TILE), 1)
    cmask = jnp.where(((rt >> 6) == (ct >> 6)) & (ct <= rt), 1.0, 0.0).astype(BF16)
    b = _masked_sum(cmask, g)
    emit()

    nchunk = TILE // CHUNK
    blast_rows = jnp.concatenate(
        [jnp.broadcast_to(b[c * CHUNK + CHUNK - 1:(c + 1) * CHUNK], (CHUNK, D_QK))
         for c in range(nchunk)], axis=0)
    k_all = p_ref[slot, :, C_QK + D_QK:C_GATE]
    k_d = k_all * jnp.exp(blast_rows - b)
    v_all = pv_ref[slot]
    tok_chunk = _iota((TILE, 2 * DK), 0) >> 6
    incr = []
    for pair in range(HEADS // 2):
        kp = k_d[:, pair * 2 * DK:(pair + 1) * 2 * DK]
        lhs_t = jnp.concatenate(
            [jnp.where(tok_chunk == c, kp, 0.0) for c in range(nchunk)], axis=1).astype(BF16)
        incr.append(_dot_tn(lhs_t, v_all[:, pair * 2 * DV:(pair + 1) * 2 * DV]))

    ar = _iota((HEADS * CHUNK, CHUNK), 0) & (CHUNK - 1)
    ac = _iota((HEADS * CHUNK, CHUNK), 1)
    causal = ac <= ar

    state = list(state)
    o_chunks = []
    for c in range(nchunk):
        r0 = c * CHUNK
        bc = b[r0:r0 + CHUNK]
        qc = p_ref[slot, r0:r0 + CHUNK, C_QK:C_QK + D_QK] * Q_SCALE
        kc = p_ref[slot, r0:r0 + CHUNK, C_QK + D_QK:C_GATE]
        vc = pv_ref[slot, r0:r0 + CHUNK, :]
        bmid = bc[CHUNK // 2:CHUNK // 2 + 1]
        blast = bc[CHUNK - 1:CHUNK]
        q_in = (qc * jnp.exp(bc)).astype(BF16)
        q_a = qc * jnp.exp(bc - bmid)
        k_a = (kc * jnp.exp(bmid - bc)).astype(BF16)

        a_all = _dot_nt(_head_stack(q_a).astype(BF16), k_a)
        a_all = jnp.where(causal, a_all, 0.0).astype(BF16)
        for hd in range(HEADS):
            sbd_ref[c, hd * DK:(hd + 1) * DK, hd * DV:(hd + 1) * DV] = state[hd].astype(BF16)
        inter = _dot(q_in, sbd_ref[c])
        intra = jnp.concatenate(
            [_dot(a_all[hd * CHUNK:(hd + 1) * CHUNK], vc[:, hd * DV:(hd + 1) * DV])
             for hd in range(HEADS)], axis=1)
        o_chunks.append(inter + intra)

        dec = jnp.transpose(jnp.broadcast_to(jnp.exp(blast), (DV, D_QK)))
        for hd in range(HEADS):
            rows = slice(hd * DK, (hd + 1) * DK)
            pair, sub = divmod(hd, 2)
            i0 = c * 2 * DK + sub * DK
            state[hd] = state[hd] * dec[rows] + incr[pair][i0:i0 + DK, sub * DV:(sub + 1) * DV]
        emit()

    o = jnp.concatenate(o_chunks, axis=0)
    mix = []
    for hd in range(HEADS):
        cols = slice(hd * DV, (hd + 1) * DV)
        gate = p_ref[slot, :, C_GATE + hd * DV:C_GATE + (hd + 1) * DV]
        mix.append((_rmsnorm(o[:, cols], gng_ref[...]) * _silu(gate)).astype(BF16))
        emit()

    cw = cw_ref[...]
    row = _iota((TILE, 256), 0)
    new_tail = []
    for half in range(D_CONV // 256):
        cols = slice(half * 256, (half + 1) * 256)

        def conv_in(k, cols=cols):
            return p_ref[slot, :, C_CONV + k * D_CONV + cols.start:C_CONV + k * D_CONV + cols.stop]

        hc = conv_in(2) * conv_in(0)
        prev = tail[:, cols]
        h1 = jnp.where(row == 0, prev[7:8], pltpu.roll(hc, 1, 0))
        h2 = jnp.where(row == 0, prev[6:7], jnp.where(row == 1, prev[7:8], pltpu.roll(hc, 2, 0)))
        yc = cw[0:1, cols] * h2 + cw[1:2, cols] * h1 + cw[2:3, cols] * hc
        mix.append((conv_in(1) * yc * _silu(conv_in(3))).astype(BF16))
        new_tail.append(hc[TILE - 8:TILE])
        cout_ref[:, cols] = hc[TILE - 2:TILE]
        emit()

    out = p_ref[slot, :, C_X:C_QK] + _dot(jnp.concatenate(mix, axis=1), wo_ref[...])
    emit()
    half_rows = TILE // 2
    y_ref[0:half_rows] = _rmsnorm(out[0:half_rows], fg_ref[...])
    emit()
    y_ref[half_rows:TILE] = _rmsnorm(out[half_rows:TILE], fg_ref[...])
    return state, jnp.concatenate(new_tail, axis=1)


def _prompt_kernel(pairs_per_seq, xa_ref, xb_ref, ng_ref, wt_ref, wup_ref, bgk_ref, gng_ref, cw_ref,
                   wo_ref, fg_ref,
                   y_ref, sout_ref, cout_ref,
                   p_ref, pv_ref, s_ref, sbd_ref, tail_ref):
    step = pl.program_id(0)
    nchunk = TILE // CHUNK

    @pl.when(step == 0)
    def _():
        p_ref[0] = jnp.zeros(p_ref.shape[1:], F32)
        pv_ref[0] = jnp.zeros(pv_ref.shape[1:], BF16)
        sbd_ref[...] = jnp.zeros_like(sbd_ref)

    @pl.when(jnp.logical_or(step == 0, (step - 1) % pairs_per_seq == 0))
    def _():
        s_ref[...] = jnp.zeros_like(s_ref)
        tail_ref[...] = jnp.zeros_like(tail_ref)

    def half(x_ref, write_slot, read_slot, y_rows, sbd_slabs, state, tail, schedule):
        items = _project_items(x_ref, ng_ref, wt_ref, p_ref, pv_ref, write_slot)
        counts = iter(schedule)
        items[0]()
        pending = iter(items[1:])

        def emit():
            for _ in range(next(counts)):
                next(pending)()

        out = _finish_tile(p_ref, pv_ref, read_slot, wup_ref, bgk_ref, gng_ref, cw_ref, wo_ref, fg_ref,
                           y_ref.at[y_rows], cout_ref, sbd_ref.at[sbd_slabs], state, tail, emit)
        assert next(counts, None) is None and next(pending, None) is None
        return out

    state = [s_ref[hd] for hd in range(HEADS)]
    tail = tail_ref[...]
    state, tail = half(xa_ref, 1, 0, pl.ds(0, TILE), pl.ds(0, nchunk), state, tail, ITEM_SCHEDULE)
    state, tail = half(xb_ref, 0, 1, pl.ds(TILE, TILE), pl.ds(nchunk, nchunk), state, tail, ITEM_SCHEDULE_B)
    for hd in range(HEADS):
        s_ref[hd] = state[hd]
        sout_ref[hd] = state[hd]
    tail_ref[...] = tail


def _sample_kernel(x_ref, cprev_ref, sin_ref, ng_ref, w_ref, wup_ref, bgk_ref,
                   gng_ref, cw_ref, wo_ref, fg_ref,
                   y_ref, sout_ref, cout_ref,
                   q4_ref, inter_ref):
    nt = SEQ_BLK * DEC_LEN
    gt = GRP * DEC_LEN
    n_grp = SEQ_BLK // GRP

    def seqs(g):
        return slice(g * GRP, (g + 1) * GRP)

    x = jnp.concatenate([x_ref[seqs(g), t, :] for g in range(n_grp) for t in range(DEC_LEN)], axis=0)
    h = _rmsnorm(x, ng_ref[...]).astype(BF16)

    qk = _proj(h, w_ref, R_QK, R_V)
    v = _proj(h, w_ref, R_V, R_GATE).astype(BF16)
    gate = _proj(h, w_ref, R_GATE, R_CONV)
    g = _gate_log_decay(_proj(h, w_ref, R_LR, R_LR + RANK), wup_ref, bgk_ref)

    def tok(i):
        return (i >> 4) & (DEC_LEN - 1)

    def same_seq(r, c):
        return ((r >> 6) == (c >> 6)) & ((r & (GRP - 1)) == (c & (GRP - 1)))

    rt = _iota((nt, nt), 0)
    ct = _iota((nt, nt), 1)
    same = same_seq(rt, ct)
    cmask = jnp.where(same & (tok(ct) <= tok(rt)), 1.0, 0.0).astype(BF16)
    fmask = jnp.where(same, 1.0, 0.0).astype(BF16)
    b = _masked_sum(cmask, g)
    bl = _masked_sum(fmask, g)

    q = qk[:, 0:D_QK] * Q_SCALE
    k = qk[:, D_QK:2 * D_QK]
    q_in = (q * jnp.exp(b)).astype(BF16)
    k_a = (k * jnp.exp(-b)).astype(BF16)
    k_d = k * jnp.exp(bl - b)
    k_dt = jnp.transpose(k_d).astype(BF16)
    dec_t = jnp.transpose(jnp.exp(bl))

    ar = _iota((HEADS * gt, gt), 0) & (gt - 1)
    ac = _iota((HEADS * gt, gt), 1)
    amask = same_seq(ar, ac) & (tok(ac) <= tok(ar))
    for gi in range(n_grp):
        stacked = _head_stack((q * jnp.exp(b))[gi * gt:(gi + 1) * gt])
        for lb in range(D_QK // 128):
            q4_ref[gi, lb] = stacked[:, lb * 128:(lb + 1) * 128]
    for j in range(SEQ_BLK):
        gi, sl = divmod(j, GRP)
        lhs = jnp.concatenate(
            [q4_ref[gi, lb, pl.ds(sl, HEADS * DEC_LEN, stride=GRP), :] for lb in range(D_QK // 128)],
            axis=1).astype(BF16)
        res = _dot(lhs, sin_ref[j].reshape(D_QK, DV).astype(BF16))
        for hd in range(HEADS):
            inter_ref[hd, pl.ds(gi * gt + sl, DEC_LEN, stride=GRP), :] = res[hd * DEC_LEN:(hd + 1) * DEC_LEN]

    o_groups = []
    for gi in range(n_grp):
        r0 = gi * gt
        qg = q_in[r0:r0 + gt]
        a_all = _dot_nt(_head_stack(qg), k_a[r0:r0 + gt])
        a_all = jnp.where(amask, a_all, 0.0).astype(BF16)
        outs = []
        for hd in range(HEADS):
            rows = slice(hd * gt, (hd + 1) * gt)
            outs.append(_dot(a_all[rows], v[r0:r0 + gt, hd * DV:(hd + 1) * DV]))
        o_groups.append(jnp.concatenate(outs, axis=1))
    o = jnp.concatenate([inter_ref[hd] for hd in range(HEADS)], axis=1) + jnp.concatenate(o_groups, axis=0)
    o = _gla_epilogue(o, gate, gng_ref)

    ur = _iota((SEQ_BLK * DK, nt), 0) >> 6
    uc = _iota((SEQ_BLK * DK, nt), 1)
    umask = ur == (((uc >> 6) << 4) | (uc & (GRP - 1)))
    for hd in range(HEADS):
        kt = jnp.tile(k_dt[hd * DK:(hd + 1) * DK], (SEQ_BLK, 1))
        u_h = _dot(jnp.where(umask, kt, 0.0), v[:, hd * DV:(hd + 1) * DV])
        for j in range(SEQ_BLK):
            col = (j // GRP) * gt + j % GRP
            dec = jnp.broadcast_to(dec_t[hd * DK:(hd + 1) * DK, col:col + 1], (DK, DV))
            sout_ref[j, hd] = sin_ref[j, hd] * dec + u_h[j * DK:(j + 1) * DK]

    conv = _proj(h, w_ref, R_CONV, R_LR)
    hc = conv[:, 2 * D_CONV:3 * D_CONV] * conv[:, 0:D_CONV]
    h1, h2 = [], []
    for gi in range(n_grp):
        r0 = gi * gt
        c0 = cprev_ref[seqs(gi), 0, :]
        c1 = cprev_ref[seqs(gi), 1, :]
        h1 += [c1, hc[r0:r0 + gt - GRP]]
        h2 += [c0, c1, hc[r0:r0 + gt - 2 * GRP]]
        cout_ref[seqs(gi), 0, :] = hc[r0 + gt - 2 * GRP:r0 + gt - GRP]
        cout_ref[seqs(gi), 1, :] = hc[r0 + gt - GRP:r0 + gt]
    cw = cw_ref[...]
    yc = cw[0:1] * jnp.concatenate(h2, axis=0) + cw[1:2] * jnp.concatenate(h1, axis=0) + cw[2:3] * hc
    yc = conv[:, D_CONV:2 * D_CONV] * yc * _silu(conv[:, 3 * D_CONV:4 * D_CONV])

    y = _out_proj(x, o, yc, wo_ref, fg_ref)
    for gi in range(n_grp):
        for t in range(DEC_LEN):
            r0 = gi * gt + t * GRP
            y_ref[seqs(gi), t, :] = y[r0:r0 + GRP]


def _const_spec(shape):
    return pl.BlockSpec(shape, lambda *_: (0,) * len(shape))


def _weight_specs():
    return [
        _const_spec((1, D_MODEL)),
        _const_spec((D_MODEL, N_W)),
        _const_spec((RANK, D_QK)),
        _const_spec((1, D_QK)),
        _const_spec((1, DV)),
        _const_spec((3, D_CONV)),
        _const_spec((D_MODEL, D_MODEL)),
        _const_spec((1, D_MODEL)),
    ]


def _prep_kernel(wt_ref, wo_ref, w_ref, wob_ref):
    w_ref[...] = jnp.transpose(wt_ref[...]).astype(BF16)
    wob_ref[...] = wo_ref[...].astype(BF16)


def _prepare_weights(w_in, w_out):
    n_qkvg = SRC_LR // PREP_BLK
    n_conv = (4 * D_CONV) // PREP_BLK
    n_blk = N_W // PREP_BLK
    assert n_blk == n_qkvg + n_conv + 1

    def src_row(i):
        row = jnp.where(i < n_qkvg, PREP_BLK * i,
                        jnp.where(i < n_qkvg + n_conv, SRC_CONV + PREP_BLK * (i - n_qkvg), SRC_LR))
        return pl.multiple_of(row, RANK)

    wo_rows = D_MODEL // n_blk
    return pl.pallas_call(
        _prep_kernel,
        grid=(n_blk,),
        in_specs=[
            pl.BlockSpec((pl.Element(PREP_BLK), pl.Element(D_MODEL)), lambda i: (src_row(i), 0)),
            pl.BlockSpec((wo_rows, D_MODEL), lambda i: (i, 0)),
        ],
        out_specs=[
            pl.BlockSpec((D_MODEL, PREP_BLK), lambda i: (0, i)),
            pl.BlockSpec((wo_rows, D_MODEL), lambda i: (i, 0)),
        ],
        out_shape=[
            jax.ShapeDtypeStruct((D_MODEL, N_W), BF16),
            jax.ShapeDtypeStruct((D_MODEL, D_MODEL), BF16),
        ],
        compiler_params=pltpu.CompilerParams(dimension_semantics=("arbitrary",)),
        name="weight_prep",
    )(jnp.swapaxes(w_in[0], 0, 1), w_out[0])


def kernel(x_prompt, x_sample, state_gla, state_conv, norm_gain, w_in, w_gk_up, b_gk,
           gla_norm_gain, conv_w, w_out, final_norm_gain):
    n_batch, seq_len, _ = x_prompt.shape
    n_dec = x_sample.shape[0]
    tiles_per_seq = seq_len // TILE
    n_tiles = n_batch * tiles_per_seq
    w_proj, w_o = _prepare_weights(w_in, w_out)
    weights = (
        norm_gain.reshape(1, D_MODEL),
        w_proj,
        w_gk_up[0].astype(BF16),
        b_gk.reshape(1, D_QK),
        gla_norm_gain.reshape(1, DV),
        conv_w[0],
        w_o,
        final_norm_gain.reshape(1, D_MODEL),
    )

    pairs_per_seq = tiles_per_seq // 2

    def tile_index(t):
        t = jnp.clip(t, 0, n_tiles - 1)
        return (t // tiles_per_seq, t % tiles_per_seq, 0)

    def out_pair(j):
        p = jnp.maximum(j - 1, 0)
        return (p // pairs_per_seq, p % pairs_per_seq, 0)

    def out_seq(j):
        return jnp.maximum(j - 1, 0) // pairs_per_seq

    y_p, s_p, c_p = pl.pallas_call(
        functools.partial(_prompt_kernel, pairs_per_seq),
        grid=(n_tiles // 2 + 1,),
        in_specs=[pl.BlockSpec((None, TILE, D_MODEL), lambda j: tile_index(2 * j - 1)),
                  pl.BlockSpec((None, TILE, D_MODEL), lambda j: tile_index(2 * j))] + _weight_specs(),
        out_specs=[
            pl.BlockSpec((None, 2 * TILE, D_MODEL), out_pair),
            pl.BlockSpec((None, None, HEADS, DK, DV), lambda j: (0, out_seq(j), 0, 0, 0)),
            pl.BlockSpec((None, None, 2, D_CONV), lambda j: (0, out_seq(j), 0, 0)),
        ],
        out_shape=[
            jax.ShapeDtypeStruct((n_batch, seq_len, D_MODEL), F32),
            jax.ShapeDtypeStruct((1, n_batch, HEADS, DK, DV), F32),
            jax.ShapeDtypeStruct((1, n_batch, 2, D_CONV), F32),
        ],
        scratch_shapes=[
            pltpu.VMEM((2, TILE, C_END), F32),
            pltpu.VMEM((2, TILE, D_GLA), BF16),
            pltpu.VMEM((HEADS, DK, DV), F32),
            pltpu.VMEM((2 * (TILE // CHUNK), D_QK, D_GLA), BF16),
            pltpu.VMEM((8, D_CONV), F32),
        ],
        compiler_params=pltpu.CompilerParams(
            dimension_semantics=("arbitrary",), vmem_limit_bytes=VMEM_LIMIT),
        name="gla_conv_prompt",
    )(x_prompt, x_prompt, *weights)

    y_s, s_s, c_s = pl.pallas_call(
        _sample_kernel,
        grid=(n_dec // SEQ_BLK,),
        in_specs=[
            pl.BlockSpec((SEQ_BLK, DEC_LEN, D_MODEL), lambda i: (i, 0, 0)),
            pl.BlockSpec((None, SEQ_BLK, 2, D_CONV), lambda i: (0, i, 0, 0)),
            pl.BlockSpec((None, SEQ_BLK, HEADS, DK, DV), lambda i: (0, i, 0, 0, 0)),
        ] + _weight_specs(),
        out_specs=[
            pl.BlockSpec((SEQ_BLK, DEC_LEN, D_MODEL), lambda i: (i, 0, 0)),
            pl.BlockSpec((None, SEQ_BLK, HEADS, DK, DV), lambda i: (0, i, 0, 0, 0)),
            pl.BlockSpec((None, SEQ_BLK, 2, D_CONV), lambda i: (0, i, 0, 0)),
        ],
        out_shape=[
            jax.ShapeDtypeStruct((n_dec, DEC_LEN, D_MODEL), F32),
            jax.ShapeDtypeStruct((1, n_dec, HEADS, DK, DV), F32),
            jax.ShapeDtypeStruct((1, n_dec, 2, D_CONV), F32),
        ],
        scratch_shapes=[
            pltpu.VMEM((SEQ_BLK // GRP, D_QK // 128, HEADS * GRP * DEC_LEN, 128), F32),
            pltpu.VMEM((HEADS, SEQ_BLK * DEC_LEN, DV), F32),
        ],
        compiler_params=pltpu.CompilerParams(
            dimension_semantics=("arbitrary",), vmem_limit_bytes=VMEM_LIMIT),
        name="gla_conv_sample",
    )(x_sample, state_conv, state_gla, *weights)

    return (y_p, y_s, s_p, c_p, s_s, c_s)
```

```python
import functools

import jax
import jax.numpy as jnp
from jax import lax
from jax.experimental import pallas as pl
from jax.experimental.pallas import tpu as pltpu

D_MODEL = 1024
HEADS = 4
DK = 64
DV = 128
D_QK = HEADS * DK
D_GLA = HEADS * DV
D_CONV = 512
RANK = 16
CHUNK = 64
TILE = 256
DEC_LEN = 4
SEQ_BLK = 32
GRP = 16
LANES = 128
MXU_COLS = 256
DK_SHIFT = 6
GRP_SHIFT = 4
GT_SHIFT = 6
EPS = 1e-6
Q_SCALE = DK ** -0.5
GATE_SCALE = 1.0 / 16.0
VMEM_LIMIT = 52 * 1024 * 1024

R_QK = 0
R_V = 2 * D_QK
R_GATE = R_V + D_GLA
R_CONV = R_GATE + D_GLA
R_LR = R_CONV + 4 * D_CONV
PREP_BLK = 512
N_W = R_LR + PREP_BLK
SRC_LR = 2 * D_QK + 2 * D_GLA
SRC_CONV = SRC_LR + RANK

C_X = 0
C_QK = C_X + D_MODEL
C_GATE = C_QK + 2 * D_QK
C_CONV = C_GATE + D_GLA
C_LR = C_CONV + 4 * D_CONV
C_END = C_LR + LANES

F32 = jnp.float32
BF16 = jnp.bfloat16


def _dot(a, b):
    return jnp.dot(a, b, preferred_element_type=F32)


def _dot_nt(a, b):
    return lax.dot_general(a, b, (((1,), (1,)), ((), ())), preferred_element_type=F32)


def _proj(h, w_ref, lo, hi):
    return _dot(h, w_ref[:, lo:hi])


def _dot_tn(a, b):
    return lax.dot_general(a, b, (((0,), (0,)), ((), ())), preferred_element_type=F32)


def _rmsnorm(x, gain):
    ms = jnp.mean(x * x, axis=-1, keepdims=True)
    return x * lax.rsqrt(ms + EPS) * gain


def _silu(x):
    return x * (1.0 / (1.0 + jnp.exp(-x)))


def _log_sigmoid(z):
    return jnp.minimum(z, 0.0) - jnp.log(1.0 + jnp.exp(-jnp.abs(z)))


def _iota(shape, dim):
    return lax.broadcasted_iota(jnp.int32, shape, dim)


def _masked_sum(mask_bf16, g):
    g1 = g.astype(BF16)
    r1 = g - g1.astype(F32)
    g2 = r1.astype(BF16)
    g3 = (r1 - g2.astype(F32)).astype(BF16)
    return _dot(mask_bf16, g1) + _dot(mask_bf16, g2) + _dot(mask_bf16, g3)


def _head_stack(q):
    lane_head = _iota(q.shape, 1) >> DK_SHIFT
    return jnp.concatenate([jnp.where(lane_head == h, q, 0.0) for h in range(HEADS)], axis=0)


def _gate_log_decay(lr, wup_ref, bgk_ref):
    z = _dot(lr.astype(BF16), wup_ref[...]) + bgk_ref[...]
    return _log_sigmoid(z) * GATE_SCALE


def _gla_epilogue(o, gate, gng_ref):
    outs = []
    for hd in range(HEADS):
        oh = o[:, hd * DV:(hd + 1) * DV]
        outs.append(_rmsnorm(oh, gng_ref[...]))
    return jnp.concatenate(outs, axis=1) * _silu(gate)


def _out_proj(x, o, yc, wo_ref, fg_ref):
    mix = jnp.concatenate([o, yc], axis=1).astype(BF16)
    out = x + _dot(mix, wo_ref[...])
    return _rmsnorm(out, fg_ref[...])


ITEM_SCHEDULE = (0, 4, 0, 1, 0, 0, 0, 1, 0, 0, 0, 6, 3, 0)
ITEM_SCHEDULE_B = (0, 4, 0, 0, 1, 1, 0, 0, 1, 0, 0, 6, 2, 0)


def _project_items(x_ref, ng_ref, wt_ref, p_ref, pv_ref, slot):
    cache = {}

    def norm():
        x = x_ref[...]
        cache["h"] = _rmsnorm(x, ng_ref[...]).astype(BF16)
        p_ref[slot, :, C_X:C_QK] = x

    def to_p(row, col, width=MXU_COLS):
        def item():
            p_ref[slot, :, col:col + width] = _proj(cache["h"], wt_ref, row, row + width)
        return item

    def to_pv(off):
        def item():
            pv_ref[slot, :, off:off + MXU_COLS] = _proj(
                cache["h"], wt_ref, R_V + off, R_V + off + MXU_COLS).astype(BF16)
        return item

    items = [norm, to_p(R_LR, C_LR, RANK)]
    items += [to_p(R_QK + o, C_QK + o) for o in (0, MXU_COLS)]
    items += [to_pv(o) for o in (0, MXU_COLS)]
    items += [to_p(R_CONV + o, C_CONV + o) for o in range(0, 4 * D_CONV, MXU_COLS)]
    items += [to_p(R_GATE + o, C_GATE + o) for o in (0, MXU_COLS)]
    return items


def _finish_tile(p_ref, pv_ref, slot, wup_ref, bgk_ref, gng_ref, cw_ref, wo_ref, fg_ref,
                 y_ref, cout_ref, sbd_ref, state, tail, emit):
    g = _gate_log_decay(p_ref[slot, :, C_LR:C_LR + RANK], wup_ref, bgk_ref)
    emit()

    rt = _iota((TILE, TILE), 0)
    ct = _iota((TILE, TILE), 1)
    cmask = jnp.where(((rt >> DK_SHIFT) == (ct >> DK_SHIFT)) & (ct <= rt), 1.0, 0.0).astype(BF16)
    b = _masked_sum(cmask, g)
    emit()

    nchunk = TILE // CHUNK
    blast_rows = jnp.concatenate(
        [jnp.broadcast_to(b[c * CHUNK + CHUNK - 1:(c + 1) * CHUNK], (CHUNK, D_QK))
         for c in range(nchunk)], axis=0)
    k_all = p_ref[slot, :, C_QK + D_QK:C_GATE]
    k_d = k_all * jnp.exp(blast_rows - b)
    v_all = pv_ref[slot]
    tok_chunk = _iota((TILE, 2 * DK), 0) >> DK_SHIFT
    incr = []
    for pair in range(HEADS // 2):
        kp = k_d[:, pair * 2 * DK:(pair + 1) * 2 * DK]
        lhs_t = jnp.concatenate(
            [jnp.where(tok_chunk == c, kp, 0.0) for c in range(nchunk)], axis=1).astype(BF16)
        incr.append(_dot_tn(lhs_t, v_all[:, pair * 2 * DV:(pair + 1) * 2 * DV]))

    ar = _iota((HEADS * CHUNK, CHUNK), 0) & (CHUNK - 1)
    ac = _iota((HEADS * CHUNK, CHUNK), 1)
    causal = ac <= ar

    state = list(state)
    o_chunks = []
    for c in range(nchunk):
        r0 = c * CHUNK
        bc = b[r0:r0 + CHUNK]
        qc = p_ref[slot, r0:r0 + CHUNK, C_QK:C_QK + D_QK] * Q_SCALE
        kc = p_ref[slot, r0:r0 + CHUNK, C_QK + D_QK:C_GATE]
        vc = pv_ref[slot, r0:r0 + CHUNK, :]
        bmid = bc[CHUNK // 2:CHUNK // 2 + 1]
        blast = bc[CHUNK - 1:CHUNK]
        q_in = (qc * jnp.exp(bc)).astype(BF16)
        q_a = qc * jnp.exp(bc - bmid)
        k_a = (kc * jnp.exp(bmid - bc)).astype(BF16)

        a_all = _dot_nt(_head_stack(q_a).astype(BF16), k_a)
        a_all = jnp.where(causal, a_all, 0.0).astype(BF16)
        for hd in range(HEADS):
            sbd_ref[c, hd * DK:(hd + 1) * DK, hd * DV:(hd + 1) * DV] = state[hd].astype(BF16)
        inter = _dot(q_in, sbd_ref[c])
        intra = jnp.concatenate(
            [_dot(a_all[hd * CHUNK:(hd + 1) * CHUNK], vc[:, hd * DV:(hd + 1) * DV])
             for hd in range(HEADS)], axis=1)
        o_chunks.append(inter + intra)

        dec = jnp.transpose(jnp.broadcast_to(jnp.exp(blast), (DV, D_QK)))
        for hd in range(HEADS):
            rows = slice(hd * DK, (hd + 1) * DK)
            pair, sub = divmod(hd, 2)
            i0 = c * 2 * DK + sub * DK
            state[hd] = state[hd] * dec[rows] + incr[pair][i0:i0 + DK, sub * DV:(sub + 1) * DV]
        emit()

    o = jnp.concatenate(o_chunks, axis=0)
    mix = []
    for hd in range(HEADS):
        cols = slice(hd * DV, (hd + 1) * DV)
        gate = p_ref[slot, :, C_GATE + hd * DV:C_GATE + (hd + 1) * DV]
        mix.append((_rmsnorm(o[:, cols], gng_ref[...]) * _silu(gate)).astype(BF16))
        emit()

    cw = cw_ref[...]
    row = _iota((TILE, MXU_COLS), 0)
    new_tail = []
    for half in range(D_CONV // MXU_COLS):
        cols = slice(half * MXU_COLS, (half + 1) * MXU_COLS)

        def conv_in(k, cols=cols):
            return p_ref[slot, :, C_CONV + k * D_CONV + cols.start:C_CONV + k * D_CONV + cols.stop]

        hc = conv_in(2) * conv_in(0)
        prev = tail[:, cols]
        h1 = jnp.where(row == 0, prev[7:8], pltpu.roll(hc, 1, 0))
        h2 = jnp.where(row == 0, prev[6:7], jnp.where(row == 1, prev[7:8], pltpu.roll(hc, 2, 0)))
        yc = cw[0:1, cols] * h2 + cw[1:2, cols] * h1 + cw[2:3, cols] * hc
        mix.append((conv_in(1) * yc * _silu(conv_in(3))).astype(BF16))
        new_tail.append(hc[TILE - 8:TILE])
        cout_ref[:, cols] = hc[TILE - 2:TILE]
        emit()

    out = p_ref[slot, :, C_X:C_QK] + _dot(jnp.concatenate(mix, axis=1), wo_ref[...])
    emit()
    half_rows = TILE // 2
    y_ref[0:half_rows] = _rmsnorm(out[0:half_rows], fg_ref[...])
    emit()
    y_ref[half_rows:TILE] = _rmsnorm(out[half_rows:TILE], fg_ref[...])
    return state, jnp.concatenate(new_tail, axis=1)


def _prompt_kernel(pairs_per_seq, xa_ref, xb_ref, ng_ref, wt_ref, wup_ref, bgk_ref, gng_ref, cw_ref,
                   wo_ref, fg_ref,
                   y_ref, sout_ref, cout_ref,
                   p_ref, pv_ref, s_ref, sbd_ref, tail_ref):
    step = pl.program_id(0)
    nchunk = TILE // CHUNK

    @pl.when(step == 0)
    def _():
        p_ref[0] = jnp.zeros(p_ref.shape[1:], F32)
        pv_ref[0] = jnp.zeros(pv_ref.shape[1:], BF16)
        sbd_ref[...] = jnp.zeros_like(sbd_ref)

    @pl.when(jnp.logical_or(step == 0, (step - 1) % pairs_per_seq == 0))
    def _():
        s_ref[...] = jnp.zeros_like(s_ref)
        tail_ref[...] = jnp.zeros_like(tail_ref)

    def half(x_ref, write_slot, read_slot, y_rows, sbd_slabs, state, tail, schedule):
        items = _project_items(x_ref, ng_ref, wt_ref, p_ref, pv_ref, write_slot)
        counts = iter(schedule)
        items[0]()
        pending = iter(items[1:])

        def emit():
            for _ in range(next(counts)):
                next(pending)()

        out = _finish_tile(p_ref, pv_ref, read_slot, wup_ref, bgk_ref, gng_ref, cw_ref, wo_ref, fg_ref,
                           y_ref.at[y_rows], cout_ref, sbd_ref.at[sbd_slabs], state, tail, emit)
        assert next(counts, None) is None and next(pending, None) is None
        return out

    state = [s_ref[hd] for hd in range(HEADS)]
    tail = tail_ref[...]
    state, tail = half(xa_ref, 1, 0, pl.ds(0, TILE), pl.ds(0, nchunk), state, tail, ITEM_SCHEDULE)
    state, tail = half(xb_ref, 0, 1, pl.ds(TILE, TILE), pl.ds(nchunk, nchunk), state, tail, ITEM_SCHEDULE_B)
    for hd in range(HEADS):
        s_ref[hd] = state[hd]
        sout_ref[hd] = state[hd]
    tail_ref[...] = tail


def _sample_kernel(x_ref, cprev_ref, sin_ref, ng_ref, w_ref, wup_ref, bgk_ref,
                   gng_ref, cw_ref, wo_ref, fg_ref,
                   y_ref, sout_ref, cout_ref,
                   q4_ref, inter_ref):
    nt = SEQ_BLK * DEC_LEN
    gt = GRP * DEC_LEN
    n_grp = SEQ_BLK // GRP

    def seqs(g):
        return slice(g * GRP, (g + 1) * GRP)

    x = jnp.concatenate([x_ref[seqs(g), t, :] for g in range(n_grp) for t in range(DEC_LEN)], axis=0)
    h = _rmsnorm(x, ng_ref[...]).astype(BF16)

    qk = _proj(h, w_ref, R_QK, R_V)
    v = _proj(h, w_ref, R_V, R_GATE).astype(BF16)
    gate = _proj(h, w_ref, R_GATE, R_CONV)
    g = _gate_log_decay(_proj(h, w_ref, R_LR, R_LR + RANK), wup_ref, bgk_ref)

    def tok(i):
        return (i >> GRP_SHIFT) & (DEC_LEN - 1)

    def same_seq(r, c):
        return ((r >> GT_SHIFT) == (c >> GT_SHIFT)) & ((r & (GRP - 1)) == (c & (GRP - 1)))

    rt = _iota((nt, nt), 0)
    ct = _iota((nt, nt), 1)
    same = same_seq(rt, ct)
    cmask = jnp.where(same & (tok(ct) <= tok(rt)), 1.0, 0.0).astype(BF16)
    fmask = jnp.where(same, 1.0, 0.0).astype(BF16)
    b = _masked_sum(cmask, g)
    bl = _masked_sum(fmask, g)

    q = qk[:, 0:D_QK] * Q_SCALE
    k = qk[:, D_QK:2 * D_QK]
    q_in = (q * jnp.exp(b)).astype(BF16)
    k_a = (k * jnp.exp(-b)).astype(BF16)
    k_d = k * jnp.exp(bl - b)
    k_dt = jnp.transpose(k_d).astype(BF16)
    dec_t = jnp.transpose(jnp.exp(bl))

    ar = _iota((HEADS * gt, gt), 0) & (gt - 1)
    ac = _iota((HEADS * gt, gt), 1)
    amask = same_seq(ar, ac) & (tok(ac) <= tok(ar))
    for gi in range(n_grp):
        stacked = _head_stack((q * jnp.exp(b))[gi * gt:(gi + 1) * gt])
        for lb in range(D_QK // LANES):
            q4_ref[gi, lb] = stacked[:, lb * LANES:(lb + 1) * LANES]
    for j in range(SEQ_BLK):
        gi, sl = divmod(j, GRP)
        lhs = jnp.concatenate(
            [q4_ref[gi, lb, pl.ds(sl, HEADS * DEC_LEN, stride=GRP), :] for lb in range(D_QK // LANES)],
            axis=1).astype(BF16)
        res = _dot(lhs, sin_ref[j].reshape(D_QK, DV).astype(BF16))
        for hd in range(HEADS):
            inter_ref[hd, pl.ds(gi * gt + sl, DEC_LEN, stride=GRP), :] = res[hd * DEC_LEN:(hd + 1) * DEC_LEN]

    o_groups = []
    for gi in range(n_grp):
        r0 = gi * gt
        qg = q_in[r0:r0 + gt]
        a_all = _dot_nt(_head_stack(qg), k_a[r0:r0 + gt])
        a_all = jnp.where(amask, a_all, 0.0).astype(BF16)
        outs = []
        for hd in range(HEADS):
            rows = slice(hd * gt, (hd + 1) * gt)
            outs.append(_dot(a_all[rows], v[r0:r0 + gt, hd * DV:(hd + 1) * DV]))
        o_groups.append(jnp.concatenate(outs, axis=1))
    o = jnp.concatenate([inter_ref[hd] for hd in range(HEADS)], axis=1) + jnp.concatenate(o_groups, axis=0)
    o = _gla_epilogue(o, gate, gng_ref)

    ur = _iota((SEQ_BLK * DK, nt), 0) >> DK_SHIFT
    uc = _iota((SEQ_BLK * DK, nt), 1)
    umask = ur == (((uc >> GT_SHIFT) << GRP_SHIFT) | (uc & (GRP - 1)))
    for hd in range(HEADS):
        kt = jnp.tile(k_dt[hd * DK:(hd + 1) * DK], (SEQ_BLK, 1))
        u_h = _dot(jnp.where(umask, kt, 0.0), v[:, hd * DV:(hd + 1) * DV])
        for j in range(SEQ_BLK):
            col = (j // GRP) * gt + j % GRP
            dec = jnp.broadcast_to(dec_t[hd * DK:(hd + 1) * DK, col:col + 1], (DK, DV))
            sout_ref[j, hd] = sin_ref[j, hd] * dec + u_h[j * DK:(j + 1) * DK]

    conv = _proj(h, w_ref, R_CONV, R_LR)
    hc = conv[:, 2 * D_CONV:3 * D_CONV] * conv[:, 0:D_CONV]
    h1, h2 = [], []
    for gi in range(n_grp):
        r0 = gi * gt
        c0 = cprev_ref[seqs(gi), 0, :]
        c1 = cprev_ref[seqs(gi), 1, :]
        h1 += [c1, hc[r0:r0 + gt - GRP]]
        h2 += [c0, c1, hc[r0:r0 + gt - 2 * GRP]]
        cout_ref[seqs(gi), 0, :] = hc[r0 + gt - 2 * GRP:r0 + gt - GRP]
        cout_ref[seqs(gi), 1, :] = hc[r0 + gt - GRP:r0 + gt]
    cw = cw_ref[...]
    yc = cw[0:1] * jnp.concatenate(h2, axis=0) + cw[1:2] * jnp.concatenate(h1, axis=0) + cw[2:3] * hc
    yc = conv[:, D_CONV:2 * D_CONV] * yc * _silu(conv[:, 3 * D_CONV:4 * D_CONV])

    y = _out_proj(x, o, yc, wo_ref, fg_ref)
    for gi in range(n_grp):
        for t in range(DEC_LEN):
            r0 = gi * gt + t * GRP
            y_ref[seqs(gi), t, :] = y[r0:r0 + GRP]


def _const_spec(shape):
    return pl.BlockSpec(shape, lambda *_: (0,) * len(shape))


def _weight_specs():
    return [
        _const_spec((1, D_MODEL)),
        _const_spec((D_MODEL, N_W)),
        _const_spec((RANK, D_QK)),
        _const_spec((1, D_QK)),
        _const_spec((1, DV)),
        _const_spec((3, D_CONV)),
        _const_spec((D_MODEL, D_MODEL)),
        _const_spec((1, D_MODEL)),
    ]


def _prep_kernel(wt_ref, wo_ref, w_ref, wob_ref):
    w_ref[...] = jnp.transpose(wt_ref[...]).astype(BF16)
    wob_ref[...] = wo_ref[...].astype(BF16)


def _prepare_weights(w_in, w_out):
    n_qkvg = SRC_LR // PREP_BLK
    n_conv = (4 * D_CONV) // PREP_BLK
    n_blk = N_W // PREP_BLK
    assert n_blk == n_qkvg + n_conv + 1

    def src_row(i):
        row = jnp.where(i < n_qkvg, PREP_BLK * i,
                        jnp.where(i < n_qkvg + n_conv, SRC_CONV + PREP_BLK * (i - n_qkvg), SRC_LR))
        return pl.multiple_of(row, RANK)

    wo_rows = D_MODEL // n_blk
    return pl.pallas_call(
        _prep_kernel,
        grid=(n_blk,),
        in_specs=[
            pl.BlockSpec((pl.Element(PREP_BLK), pl.Element(D_MODEL)), lambda i: (src_row(i), 0)),
            pl.BlockSpec((wo_rows, D_MODEL), lambda i: (i, 0)),
        ],
        out_specs=[
            pl.BlockSpec((D_MODEL, PREP_BLK), lambda i: (0, i)),
            pl.BlockSpec((wo_rows, D_MODEL), lambda i: (i, 0)),
        ],
        out_shape=[
            jax.ShapeDtypeStruct((D_MODEL, N_W), BF16),
            jax.ShapeDtypeStruct((D_MODEL, D_MODEL), BF16),
        ],
        compiler_params=pltpu.CompilerParams(dimension_semantics=("arbitrary",)),
        name="weight_prep",
    )(jnp.swapaxes(w_in[0], 0, 1), w_out[0])


def kernel(x_prompt, x_sample, state_gla, state_conv, norm_gain, w_in, w_gk_up, b_gk,
           gla_norm_gain, conv_w, w_out, final_norm_gain):
    n_batch, seq_len, _ = x_prompt.shape
    n_dec = x_sample.shape[0]
    tiles_per_seq = seq_len // TILE
    n_tiles = n_batch * tiles_per_seq
    w_proj, w_o = _prepare_weights(w_in, w_out)
    weights = (
        norm_gain.reshape(1, D_MODEL),
        w_proj,
        w_gk_up[0].astype(BF16),
        b_gk.reshape(1, D_QK),
        gla_norm_gain.reshape(1, DV),
        conv_w[0],
        w_o,
        final_norm_gain.reshape(1, D_MODEL),
    )

    pairs_per_seq = tiles_per_seq // 2

    def tile_index(t):
        t = jnp.clip(t, 0, n_tiles - 1)
        return (t // tiles_per_seq, t % tiles_per_seq, 0)

    def out_pair(j):
        p = jnp.maximum(j - 1, 0)
        return (p // pairs_per_seq, p % pairs_per_seq, 0)

    def out_seq(j):
        return jnp.maximum(j - 1, 0) // pairs_per_seq

    y_p, s_p, c_p = pl.pallas_call(
        functools.partial(_prompt_kernel, pairs_per_seq),
        grid=(n_tiles // 2 + 1,),
        in_specs=[pl.BlockSpec((None, TILE, D_MODEL), lambda j: tile_index(2 * j - 1)),
                  pl.BlockSpec((None, TILE, D_MODEL), lambda j: tile_index(2 * j))] + _weight_specs(),
        out_specs=[
            pl.BlockSpec((None, 2 * TILE, D_MODEL), out_pair),
            pl.BlockSpec((None, None, HEADS, DK, DV), lambda j: (0, out_seq(j), 0, 0, 0)),
            pl.BlockSpec((None, None, 2, D_CONV), lambda j: (0, out_seq(j), 0, 0)),
        ],
        out_shape=[
            jax.ShapeDtypeStruct((n_batch, seq_len, D_MODEL), F32),
            jax.ShapeDtypeStruct((1, n_batch, HEADS, DK, DV), F32),
            jax.ShapeDtypeStruct((1, n_batch, 2, D_CONV), F32),
        ],
        scratch_shapes=[
            pltpu.VMEM((2, TILE, C_END), F32),
            pltpu.VMEM((2, TILE, D_GLA), BF16),
            pltpu.VMEM((HEADS, DK, DV), F32),
            pltpu.VMEM((2 * (TILE // CHUNK), D_QK, D_GLA), BF16),
            pltpu.VMEM((8, D_CONV), F32),
        ],
        compiler_params=pltpu.CompilerParams(
            dimension_semantics=("arbitrary",), vmem_limit_bytes=VMEM_LIMIT),
        name="gla_conv_prompt",
    )(x_prompt, x_prompt, *weights)

    y_s, s_s, c_s = pl.pallas_call(
        _sample_kernel,
        grid=(n_dec // SEQ_BLK,),
        in_specs=[
            pl.BlockSpec((SEQ_BLK, DEC_LEN, D_MODEL), lambda i: (i, 0, 0)),
            pl.BlockSpec((None, SEQ_BLK, 2, D_CONV), lambda i: (0, i, 0, 0)),
            pl.BlockSpec((None, SEQ_BLK, HEADS, DK, DV), lambda i: (0, i, 0, 0, 0)),
        ] + _weight_specs(),
        out_specs=[
            pl.BlockSpec((SEQ_BLK, DEC_LEN, D_MODEL), lambda i: (i, 0, 0)),
            pl.BlockSpec((None, SEQ_BLK, HEADS, DK, DV), lambda i: (0, i, 0, 0, 0)),
            pl.BlockSpec((None, SEQ_BLK, 2, D_CONV), lambda i: (0, i, 0, 0)),
        ],
        out_shape=[
            jax.ShapeDtypeStruct((n_dec, DEC_LEN, D_MODEL), F32),
            jax.ShapeDtypeStruct((1, n_dec, HEADS, DK, DV), F32),
            jax.ShapeDtypeStruct((1, n_dec, 2, D_CONV), F32),
        ],
        scratch_shapes=[
            pltpu.VMEM((SEQ_BLK // GRP, D_QK // LANES, HEADS * GRP * DEC_LEN, LANES), F32),
            pltpu.VMEM((HEADS, SEQ_BLK * DEC_LEN, DV), F32),
        ],
        compiler_params=pltpu.CompilerParams(
            dimension_semantics=("arbitrary",), vmem_limit_bytes=VMEM_LIMIT),
        name="gla_conv_sample",
    )(x_sample, state_conv, state_gla, *weights)

    return (y_p, y_s, s_p, c_p, s_s, c_s)
```

```python
import functools

import jax
import jax.numpy as jnp
from jax import lax
from jax.experimental import pallas as pl
from jax.experimental.pallas import tpu as pltpu

D_MODEL = 1024
HEADS = 4
DK = 64
DV = 128
D_QK = HEADS * DK
D_GLA = HEADS * DV
D_CONV = 512
RANK = 16
CHUNK = 64
TILE = 256
DEC_LEN = 4
SEQ_BLK = 32
GRP = 16
LANES = 128
MXU_COLS = 256
DK_SHIFT = 6
GRP_SHIFT = 4
GT_SHIFT = 6
EPS = 1e-6
Q_SCALE = DK ** -0.5
GATE_SCALE = 1.0 / 16.0
VMEM_LIMIT = 52 * 1024 * 1024

R_QK = 0
R_V = 2 * D_QK
R_GATE = R_V + D_GLA
R_CONV = R_GATE + D_GLA
R_LR = R_CONV + 4 * D_CONV
PREP_BLK = 512
N_W = R_LR + PREP_BLK
SRC_LR = 2 * D_QK + 2 * D_GLA
SRC_CONV = SRC_LR + RANK

C_X = 0
C_QK = C_X + D_MODEL
C_GATE = C_QK + 2 * D_QK
C_CONV = C_GATE + D_GLA
C_LR = C_CONV + 4 * D_CONV
C_END = C_LR + LANES

F32 = jnp.float32
BF16 = jnp.bfloat16


def _dot(a, b):
    return jnp.dot(a, b, preferred_element_type=F32)


def _dot_nt(a, b):
    return lax.dot_general(a, b, (((1,), (1,)), ((), ())), preferred_element_type=F32)


def _proj(h, w_ref, lo, hi):
    pieces = []
    while lo < hi:
        blk, off = divmod(lo, PREP_BLK)
        width = min(hi - lo, PREP_BLK - off)
        pieces.append(_dot(h, w_ref[blk, :, off:off + width]))
        lo += width
    return pieces[0] if len(pieces) == 1 else jnp.concatenate(pieces, axis=1)


def _dot_tn(a, b):
    return lax.dot_general(a, b, (((0,), (0,)), ((), ())), preferred_element_type=F32)


def _rmsnorm(x, gain):
    ms = jnp.mean(x * x, axis=-1, keepdims=True)
    return x * lax.rsqrt(ms + EPS) * gain


def _silu(x):
    return x * (1.0 / (1.0 + jnp.exp(-x)))


def _log_sigmoid(z):
    return jnp.minimum(z, 0.0) - jnp.log(1.0 + jnp.exp(-jnp.abs(z)))


def _iota(shape, dim):
    return lax.broadcasted_iota(jnp.int32, shape, dim)


def _masked_sum(mask_bf16, g):
    g1 = g.astype(BF16)
    r1 = g - g1.astype(F32)
    g2 = r1.astype(BF16)
    g3 = (r1 - g2.astype(F32)).astype(BF16)
    return _dot(mask_bf16, g1) + _dot(mask_bf16, g2) + _dot(mask_bf16, g3)


def _head_stack(q):
    lane_head = _iota(q.shape, 1) >> DK_SHIFT
    return jnp.concatenate([jnp.where(lane_head == h, q, 0.0) for h in range(HEADS)], axis=0)


def _gate_log_decay(lr, wup_ref, bgk_ref):
    z = _dot(lr.astype(BF16), wup_ref[...]) + bgk_ref[...]
    return _log_sigmoid(z) * GATE_SCALE


def _gla_epilogue(o, gate, gng_ref):
    outs = []
    for hd in range(HEADS):
        oh = o[:, hd * DV:(hd + 1) * DV]
        outs.append(_rmsnorm(oh, gng_ref[...]))
    return jnp.concatenate(outs, axis=1) * _silu(gate)


def _out_proj(x, o, yc, wo_ref, fg_ref):
    mix = jnp.concatenate([o, yc], axis=1).astype(BF16)
    out = x + _dot(mix, wo_ref[...])
    return _rmsnorm(out, fg_ref[...])


ITEM_SCHEDULE = (0, 4, 0, 1, 0, 0, 0, 1, 0, 0, 0, 6, 3, 0)
ITEM_SCHEDULE_B = (0, 4, 0, 0, 1, 1, 0, 0, 1, 0, 0, 6, 2, 0)


def _project_items(x_ref, ng_ref, wt_ref, p_ref, pv_ref, slot):
    cache = {}

    def norm():
        x = x_ref[...]
        cache["h"] = _rmsnorm(x, ng_ref[...]).astype(BF16)
        p_ref[slot, :, C_X:C_QK] = x

    def to_p(row, col, width=MXU_COLS):
        def item():
            p_ref[slot, :, col:col + width] = _proj(cache["h"], wt_ref, row, row + width)
        return item

    def to_pv(off):
        def item():
            pv_ref[slot, :, off:off + MXU_COLS] = _proj(
                cache["h"], wt_ref, R_V + off, R_V + off + MXU_COLS).astype(BF16)
        return item

    items = [norm, to_p(R_LR, C_LR, RANK)]
    items += [to_p(R_QK + o, C_QK + o) for o in (0, MXU_COLS)]
    items += [to_pv(o) for o in (0, MXU_COLS)]
    items += [to_p(R_CONV + o, C_CONV + o) for o in range(0, 4 * D_CONV, MXU_COLS)]
    items += [to_p(R_GATE + o, C_GATE + o) for o in (0, MXU_COLS)]
    return items


def _finish_tile(p_ref, pv_ref, slot, wup_ref, bgk_ref, gng_ref, cw_ref, wo_ref, fg_ref,
                 y_ref, cout_ref, sbd_ref, state, tail, emit):
    g = _gate_log_decay(p_ref[slot, :, C_LR:C_LR + RANK], wup_ref, bgk_ref)
    emit()

    rt = _iota((TILE, TILE), 0)
    ct = _iota((TILE, TILE), 1)
    cmask = jnp.where(((rt >> DK_SHIFT) == (ct >> DK_SHIFT)) & (ct <= rt), 1.0, 0.0).astype(BF16)
    b = _masked_sum(cmask, g)
    emit()

    nchunk = TILE // CHUNK
    blast_rows = jnp.concatenate(
        [jnp.broadcast_to(b[c * CHUNK + CHUNK - 1:(c + 1) * CHUNK], (CHUNK, D_QK))
         for c in range(nchunk)], axis=0)
    k_all = p_ref[slot, :, C_QK + D_QK:C_GATE]
    k_d = k_all * jnp.exp(blast_rows - b)
    v_all = pv_ref[slot]
    tok_chunk = _iota((TILE, 2 * DK), 0) >> DK_SHIFT
    incr = []
    for pair in range(HEADS // 2):
        kp = k_d[:, pair * 2 * DK:(pair + 1) * 2 * DK]
        lhs_t = jnp.concatenate(
            [jnp.where(tok_chunk == c, kp, 0.0) for c in range(nchunk)], axis=1).astype(BF16)
        incr.append(_dot_tn(lhs_t, v_all[:, pair * 2 * DV:(pair + 1) * 2 * DV]))

    ar = _iota((HEADS * CHUNK, CHUNK), 0) & (CHUNK - 1)
    ac = _iota((HEADS * CHUNK, CHUNK), 1)
    causal = ac <= ar

    state = list(state)
    o_chunks = []
    for c in range(nchunk):
        r0 = c * CHUNK
        bc = b[r0:r0 + CHUNK]
        qc = p_ref[slot, r0:r0 + CHUNK, C_QK:C_QK + D_QK] * Q_SCALE
        kc = p_ref[slot, r0:r0 + CHUNK, C_QK + D_QK:C_GATE]
        vc = pv_ref[slot, r0:r0 + CHUNK, :]
        bmid = bc[CHUNK // 2:CHUNK // 2 + 1]
        blast = bc[CHUNK - 1:CHUNK]
        q_in = (qc * jnp.exp(bc)).astype(BF16)
        q_a = qc * jnp.exp(bc - bmid)
        k_a = (kc * jnp.exp(bmid - bc)).astype(BF16)

        a_all = _dot_nt(_head_stack(q_a).astype(BF16), k_a)
        a_all = jnp.where(causal, a_all, 0.0).astype(BF16)
        for hd in range(HEADS):
            sbd_ref[c, hd * DK:(hd + 1) * DK, hd * DV:(hd + 1) * DV] = state[hd].astype(BF16)
        inter = _dot(q_in, sbd_ref[c])
        intra = jnp.concatenate(
            [_dot(a_all[hd * CHUNK:(hd + 1) * CHUNK], vc[:, hd * DV:(hd + 1) * DV])
             for hd in range(HEADS)], axis=1)
        o_chunks.append(inter + intra)

        dec = jnp.transpose(jnp.broadcast_to(jnp.exp(blast), (DV, D_QK)))
        for hd in range(HEADS):
            rows = slice(hd * DK, (hd + 1) * DK)
            pair, sub = divmod(hd, 2)
            i0 = c * 2 * DK + sub * DK
            state[hd] = state[hd] * dec[rows] + incr[pair][i0:i0 + DK, sub * DV:(sub + 1) * DV]
        emit()

    o = jnp.concatenate(o_chunks, axis=0)
    mix = []
    for hd in range(HEADS):
        cols = slice(hd * DV, (hd + 1) * DV)
        gate = p_ref[slot, :, C_GATE + hd * DV:C_GATE + (hd + 1) * DV]
        mix.append((_rmsnorm(o[:, cols], gng_ref[...]) * _silu(gate)).astype(BF16))
        emit()

    cw = cw_ref[...]
    row = _iota((TILE, MXU_COLS), 0)
    new_tail = []
    for half in range(D_CONV // MXU_COLS):
        cols = slice(half * MXU_COLS, (half + 1) * MXU_COLS)

        def conv_in(k, cols=cols):
            return p_ref[slot, :, C_CONV + k * D_CONV + cols.start:C_CONV + k * D_CONV + cols.stop]

        hc = conv_in(2) * conv_in(0)
        prev = tail[:, cols]
        h1 = jnp.where(row == 0, prev[7:8], pltpu.roll(hc, 1, 0))
        h2 = jnp.where(row == 0, prev[6:7], jnp.where(row == 1, prev[7:8], pltpu.roll(hc, 2, 0)))
        yc = cw[0:1, cols] * h2 + cw[1:2, cols] * h1 + cw[2:3, cols] * hc
        mix.append((conv_in(1) * yc * _silu(conv_in(3))).astype(BF16))
        new_tail.append(hc[TILE - 8:TILE])
        cout_ref[:, cols] = hc[TILE - 2:TILE]
        emit()

    out = p_ref[slot, :, C_X:C_QK] + _dot(jnp.concatenate(mix, axis=1), wo_ref[...])
    emit()
    half_rows = TILE // 2
    y_ref[0:half_rows] = _rmsnorm(out[0:half_rows], fg_ref[...])
    emit()
    y_ref[half_rows:TILE] = _rmsnorm(out[half_rows:TILE], fg_ref[...])
    return state, jnp.concatenate(new_tail, axis=1)


def _prompt_kernel(pairs_per_seq, xa_ref, xb_ref, ng_ref, wt_ref, wup_ref, bgk_ref, gng_ref, cw_ref,
                   wo_ref, fg_ref,
                   y_ref, sout_ref, cout_ref,
                   p_ref, pv_ref, s_ref, sbd_ref, tail_ref):
    step = pl.program_id(0)
    nchunk = TILE // CHUNK

    @pl.when(step == 0)
    def _():
        p_ref[0] = jnp.zeros(p_ref.shape[1:], F32)
        pv_ref[0] = jnp.zeros(pv_ref.shape[1:], BF16)
        sbd_ref[...] = jnp.zeros_like(sbd_ref)

    @pl.when(jnp.logical_or(step == 0, (step - 1) % pairs_per_seq == 0))
    def _():
        s_ref[...] = jnp.zeros_like(s_ref)
        tail_ref[...] = jnp.zeros_like(tail_ref)

    def half(x_ref, write_slot, read_slot, y_rows, sbd_slabs, state, tail, schedule):
        items = _project_items(x_ref, ng_ref, wt_ref, p_ref, pv_ref, write_slot)
        counts = iter(schedule)
        items[0]()
        pending = iter(items[1:])

        def emit():
            for _ in range(next(counts)):
                next(pending)()

        out = _finish_tile(p_ref, pv_ref, read_slot, wup_ref, bgk_ref, gng_ref, cw_ref, wo_ref, fg_ref,
                           y_ref.at[y_rows], cout_ref, sbd_ref.at[sbd_slabs], state, tail, emit)
        assert next(counts, None) is None and next(pending, None) is None
        return out

    state = [s_ref[hd] for hd in range(HEADS)]
    tail = tail_ref[...]
    state, tail = half(xa_ref, 1, 0, pl.ds(0, TILE), pl.ds(0, nchunk), state, tail, ITEM_SCHEDULE)
    state, tail = half(xb_ref, 0, 1, pl.ds(TILE, TILE), pl.ds(nchunk, nchunk), state, tail, ITEM_SCHEDULE_B)
    for hd in range(HEADS):
        s_ref[hd] = state[hd]
        sout_ref[hd] = state[hd]
    tail_ref[...] = tail


def _sample_kernel(x_ref, cprev_ref, sin_ref, ng_ref, w_ref, wup_ref, bgk_ref,
                   gng_ref, cw_ref, wo_ref, fg_ref,
                   y_ref, sout_ref, cout_ref,
                   q4_ref, inter_ref):
    nt = SEQ_BLK * DEC_LEN
    gt = GRP * DEC_LEN
    n_grp = SEQ_BLK // GRP

    def seqs(g):
        return slice(g * GRP, (g + 1) * GRP)

    x = jnp.concatenate([x_ref[seqs(g), t, :] for g in range(n_grp) for t in range(DEC_LEN)], axis=0)
    h = _rmsnorm(x, ng_ref[...]).astype(BF16)

    qk = _proj(h, w_ref, R_QK, R_V)
    v = _proj(h, w_ref, R_V, R_GATE).astype(BF16)
    gate = _proj(h, w_ref, R_GATE, R_CONV)
    g = _gate_log_decay(_proj(h, w_ref, R_LR, R_LR + RANK), wup_ref, bgk_ref)

    def tok(i):
        return (i >> GRP_SHIFT) & (DEC_LEN - 1)

    def same_seq(r, c):
        return ((r >> GT_SHIFT) == (c >> GT_SHIFT)) & ((r & (GRP - 1)) == (c & (GRP - 1)))

    rt = _iota((nt, nt), 0)
    ct = _iota((nt, nt), 1)
    same = same_seq(rt, ct)
    cmask = jnp.where(same & (tok(ct) <= tok(rt)), 1.0, 0.0).astype(BF16)
    fmask = jnp.where(same, 1.0, 0.0).astype(BF16)
    b = _masked_sum(cmask, g)
    bl = _masked_sum(fmask, g)

    q = qk[:, 0:D_QK] * Q_SCALE
    k = qk[:, D_QK:2 * D_QK]
    q_in = (q * jnp.exp(b)).astype(BF16)
    k_a = (k * jnp.exp(-b)).astype(BF16)
    k_d = k * jnp.exp(bl - b)
    k_dt = jnp.transpose(k_d).astype(BF16)
    dec_t = jnp.transpose(jnp.exp(bl))

    ar = _iota((HEADS * gt, gt), 0) & (gt - 1)
    ac = _iota((HEADS * gt, gt), 1)
    amask = same_seq(ar, ac) & (tok(ac) <= tok(ar))
    for gi in range(n_grp):
        stacked = _head_stack((q * jnp.exp(b))[gi * gt:(gi + 1) * gt])
        for lb in range(D_QK // LANES):
            q4_ref[gi, lb] = stacked[:, lb * LANES:(lb + 1) * LANES]
    for j in range(SEQ_BLK):
        gi, sl = divmod(j, GRP)
        lhs = jnp.concatenate(
            [q4_ref[gi, lb, pl.ds(sl, HEADS * DEC_LEN, stride=GRP), :] for lb in range(D_QK // LANES)],
            axis=1).astype(BF16)
        res = _dot(lhs, sin_ref[j].reshape(D_QK, DV).astype(BF16))
        for hd in range(HEADS):
            inter_ref[hd, pl.ds(gi * gt + sl, DEC_LEN, stride=GRP), :] = res[hd * DEC_LEN:(hd + 1) * DEC_LEN]

    o_groups = []
    for gi in range(n_grp):
        r0 = gi * gt
        qg = q_in[r0:r0 + gt]
        a_all = _dot_nt(_head_stack(qg), k_a[r0:r0 + gt])
        a_all = jnp.where(amask, a_all, 0.0).astype(BF16)
        outs = []
        for hd in range(HEADS):
            rows = slice(hd * gt, (hd + 1) * gt)
            outs.append(_dot(a_all[rows], v[r0:r0 + gt, hd * DV:(hd + 1) * DV]))
        o_groups.append(jnp.concatenate(outs, axis=1))
    o = jnp.concatenate([inter_ref[hd] for hd in range(HEADS)], axis=1) + jnp.concatenate(o_groups, axis=0)
    o = _gla_epilogue(o, gate, gng_ref)

    ur = _iota((SEQ_BLK * DK, nt), 0) >> DK_SHIFT
    uc = _iota((SEQ_BLK * DK, nt), 1)
    umask = ur == (((uc >> GT_SHIFT) << GRP_SHIFT) | (uc & (GRP - 1)))
    for hd in range(HEADS):
        kt = jnp.tile(k_dt[hd * DK:(hd + 1) * DK], (SEQ_BLK, 1))
        u_h = _dot(jnp.where(umask, kt, 0.0), v[:, hd * DV:(hd + 1) * DV])
        for j in range(SEQ_BLK):
            col = (j // GRP) * gt + j % GRP
            dec = jnp.broadcast_to(dec_t[hd * DK:(hd + 1) * DK, col:col + 1], (DK, DV))
            sout_ref[j, hd] = sin_ref[j, hd] * dec + u_h[j * DK:(j + 1) * DK]

    conv = _proj(h, w_ref, R_CONV, R_LR)
    hc = conv[:, 2 * D_CONV:3 * D_CONV] * conv[:, 0:D_CONV]
    h1, h2 = [], []
    for gi in range(n_grp):
        r0 = gi * gt
        c0 = cprev_ref[seqs(gi), 0, :]
        c1 = cprev_ref[seqs(gi), 1, :]
        h1 += [c1, hc[r0:r0 + gt - GRP]]
        h2 += [c0, c1, hc[r0:r0 + gt - 2 * GRP]]
        cout_ref[seqs(gi), 0, :] = hc[r0 + gt - 2 * GRP:r0 + gt - GRP]
        cout_ref[seqs(gi), 1, :] = hc[r0 + gt - GRP:r0 + gt]
    cw = cw_ref[...]
    yc = cw[0:1] * jnp.concatenate(h2, axis=0) + cw[1:2] * jnp.concatenate(h1, axis=0) + cw[2:3] * hc
    yc = conv[:, D_CONV:2 * D_CONV] * yc * _silu(conv[:, 3 * D_CONV:4 * D_CONV])

    y = _out_proj(x, o, yc, wo_ref, fg_ref)
    for gi in range(n_grp):
        for t in range(DEC_LEN):
            r0 = gi * gt + t * GRP
            y_ref[seqs(gi), t, :] = y[r0:r0 + GRP]


def _const_spec(shape):
    return pl.BlockSpec(shape, lambda *_: (0,) * len(shape))


def _weight_specs():
    return [
        _const_spec((1, D_MODEL)),
        _const_spec((N_W // PREP_BLK, D_MODEL, PREP_BLK)),
        _const_spec((RANK, D_QK)),
        _const_spec((1, D_QK)),
        _const_spec((1, DV)),
        _const_spec((3, D_CONV)),
        _const_spec((D_MODEL, D_MODEL)),
        _const_spec((1, D_MODEL)),
    ]


def _prep_kernel(wt_ref, wo_ref, w_ref, wob_ref):
    w_ref[...] = jnp.transpose(wt_ref[...]).astype(BF16)
    wob_ref[...] = wo_ref[...].astype(BF16)


def _prepare_weights(w_in, w_out):
    n_qkvg = SRC_LR // PREP_BLK
    n_conv = (4 * D_CONV) // PREP_BLK
    n_blk = N_W // PREP_BLK
    assert n_blk == n_qkvg + n_conv + 1

    def src_row(i):
        row = jnp.where(i < n_qkvg, PREP_BLK * i,
                        jnp.where(i < n_qkvg + n_conv, SRC_CONV + PREP_BLK * (i - n_qkvg), SRC_LR))
        return pl.multiple_of(row, RANK)

    wo_rows = D_MODEL // n_blk
    return pl.pallas_call(
        _prep_kernel,
        grid=(n_blk,),
        in_specs=[
            pl.BlockSpec((pl.Element(PREP_BLK), pl.Element(D_MODEL)), lambda i: (src_row(i), 0)),
            pl.BlockSpec((wo_rows, D_MODEL), lambda i: (i, 0)),
        ],
        out_specs=[
            pl.BlockSpec((None, D_MODEL, PREP_BLK), lambda i: (i, 0, 0)),
            pl.BlockSpec((wo_rows, D_MODEL), lambda i: (i, 0)),
        ],
        out_shape=[
            jax.ShapeDtypeStruct((N_W // PREP_BLK, D_MODEL, PREP_BLK), BF16),
            jax.ShapeDtypeStruct((D_MODEL, D_MODEL), BF16),
        ],
        compiler_params=pltpu.CompilerParams(dimension_semantics=("arbitrary",)),
        name="weight_prep",
    )(jnp.swapaxes(w_in[0], 0, 1), w_out[0])


def kernel(x_prompt, x_sample, state_gla, state_conv, norm_gain, w_in, w_gk_up, b_gk,
           gla_norm_gain, conv_w, w_out, final_norm_gain):
    n_batch, seq_len, _ = x_prompt.shape
    n_dec = x_sample.shape[0]
    tiles_per_seq = seq_len // TILE
    n_tiles = n_batch * tiles_per_seq
    w_proj, w_o = _prepare_weights(w_in, w_out)
    weights = (
        norm_gain.reshape(1, D_MODEL),
        w_proj,
        w_gk_up[0].astype(BF16),
        b_gk.reshape(1, D_QK),
        gla_norm_gain.reshape(1, DV),
        conv_w[0],
        w_o,
        final_norm_gain.reshape(1, D_MODEL),
    )

    pairs_per_seq = tiles_per_seq // 2

    def tile_index(t):
        t = jnp.clip(t, 0, n_tiles - 1)
        return (t // tiles_per_seq, t % tiles_per_seq, 0)

    def out_pair(j):
        p = jnp.maximum(j - 1, 0)
        return (p // pairs_per_seq, p % pairs_per_seq, 0)

    def out_seq(j):
        return jnp.maximum(j - 1, 0) // pairs_per_seq

    y_p, s_p, c_p = pl.pallas_call(
        functools.partial(_prompt_kernel, pairs_per_seq),
        grid=(n_tiles // 2 + 1,),
        in_specs=[pl.BlockSpec((None, TILE, D_MODEL), lambda j: tile_index(2 * j - 1)),
                  pl.BlockSpec((None, TILE, D_MODEL), lambda j: tile_index(2 * j))] + _weight_specs(),
        out_specs=[
            pl.BlockSpec((None, 2 * TILE, D_MODEL), out_pair),
            pl.BlockSpec((None, None, HEADS, DK, DV), lambda j: (0, out_seq(j), 0, 0, 0)),
            pl.BlockSpec((None, None, 2, D_CONV), lambda j: (0, out_seq(j), 0, 0)),
        ],
        out_shape=[
            jax.ShapeDtypeStruct((n_batch, seq_len, D_MODEL), F32),
            jax.ShapeDtypeStruct((1, n_batch, HEADS, DK, DV), F32),
            jax.ShapeDtypeStruct((1, n_batch, 2, D_CONV), F32),
        ],
        scratch_shapes=[
            pltpu.VMEM((2, TILE, C_END), F32),
            pltpu.VMEM((2, TILE, D_GLA), BF16),
            pltpu.VMEM((HEADS, DK, DV), F32),
            pltpu.VMEM((2 * (TILE // CHUNK), D_QK, D_GLA), BF16),
            pltpu.VMEM((8, D_CONV), F32),
        ],
        compiler_params=pltpu.CompilerParams(
            dimension_semantics=("arbitrary",), vmem_limit_bytes=VMEM_LIMIT),
        name="gla_conv_prompt",
    )(x_prompt, x_prompt, *weights)

    y_s, s_s, c_s = pl.pallas_call(
        _sample_kernel,
        grid=(n_dec // SEQ_BLK,),
        in_specs=[
            pl.BlockSpec((SEQ_BLK, DEC_LEN, D_MODEL), lambda i: (i, 0, 0)),
            pl.BlockSpec((None, SEQ_BLK, 2, D_CONV), lambda i: (0, i, 0, 0)),
            pl.BlockSpec((None, SEQ_BLK, HEADS, DK, DV), lambda i: (0, i, 0, 0, 0)),
        ] + _weight_specs(),
        out_specs=[
            pl.BlockSpec((SEQ_BLK, DEC_LEN, D_MODEL), lambda i: (i, 0, 0)),
            pl.BlockSpec((None, SEQ_BLK, HEADS, DK, DV), lambda i: (0, i, 0, 0, 0)),
            pl.BlockSpec((None, SEQ_BLK, 2, D_CONV), lambda i: (0, i, 0, 0)),
        ],
        out_shape=[
            jax.ShapeDtypeStruct((n_dec, DEC_LEN, D_MODEL), F32),
            jax.ShapeDtypeStruct((1, n_dec, HEADS, DK, DV), F32),
            jax.ShapeDtypeStruct((1, n_dec, 2, D_CONV), F32),
        ],
        scratch_shapes=[
            pltpu.VMEM((SEQ_BLK // GRP, D_QK // LANES, HEADS * GRP * DEC_LEN, LANES), F32),
            pltpu.VMEM((HEADS, SEQ_BLK * DEC_LEN, DV), F32),
        ],
        compiler_params=pltpu.CompilerParams(
            dimension_semantics=("arbitrary",), vmem_limit_bytes=VMEM_LIMIT),
        name="gla_conv_sample",
    )(x_sample, state_conv, state_gla, *weights)

    return (y_p, y_s, s_p, c_p, s_s, c_s)
```

```python
import functools

import jax
import jax.numpy as jnp
from jax import lax
from jax.experimental import pallas as pl
from jax.experimental.pallas import tpu as pltpu

D_MODEL = 1024
HEADS = 4
DK = 64
DV = 128
D_QK = HEADS * DK
D_GLA = HEADS * DV
D_CONV = 512
RANK = 16
CHUNK = 64
TILE = 256
DEC_LEN = 4
SEQ_BLK = 32
GRP = 16
LANES = 128
MXU_COLS = 256
DK_SHIFT = 6
GRP_SHIFT = 4
GT_SHIFT = 6
EPS = 1e-6
Q_SCALE = DK ** -0.5
GATE_SCALE = 1.0 / 16.0
VMEM_LIMIT = 52 * 1024 * 1024

R_QK = 0
R_V = 2 * D_QK
R_GATE = R_V + D_GLA
R_CONV = R_GATE + D_GLA
R_LR = R_CONV + 4 * D_CONV
PREP_BLK = 512
N_W = R_LR + PREP_BLK
SRC_LR = 2 * D_QK + 2 * D_GLA
SRC_CONV = SRC_LR + RANK

C_X = 0
C_QK = C_X + D_MODEL
C_GATE = C_QK + 2 * D_QK
C_CONV = C_GATE + D_GLA
C_LR = C_CONV + 4 * D_CONV
C_END = C_LR + LANES

F32 = jnp.float32
BF16 = jnp.bfloat16


def _dot(a, b):
    return jnp.dot(a, b, preferred_element_type=F32)


def _dot_nt(a, b):
    return lax.dot_general(a, b, (((1,), (1,)), ((), ())), preferred_element_type=F32)


def _proj(h, w_ref, lo, hi):
    return _dot(h, w_ref[:, lo:hi])


def _dot_tn(a, b):
    return lax.dot_general(a, b, (((0,), (0,)), ((), ())), preferred_element_type=F32)


def _rmsnorm(x, gain):
    ms = jnp.mean(x * x, axis=-1, keepdims=True)
    return x * lax.rsqrt(ms + EPS) * gain


def _silu(x):
    return x * (1.0 / (1.0 + jnp.exp(-x)))


def _log_sigmoid(z):
    return jnp.minimum(z, 0.0) - jnp.log(1.0 + jnp.exp(-jnp.abs(z)))


def _iota(shape, dim):
    return lax.broadcasted_iota(jnp.int32, shape, dim)


def _masked_sum(mask_bf16, g):
    g1 = g.astype(BF16)
    r1 = g - g1.astype(F32)
    g2 = r1.astype(BF16)
    g3 = (r1 - g2.astype(F32)).astype(BF16)
    return _dot(mask_bf16, g1) + _dot(mask_bf16, g2) + _dot(mask_bf16, g3)


def _head_stack(q):
    lane_head = _iota(q.shape, 1) >> DK_SHIFT
    return jnp.concatenate([jnp.where(lane_head == h, q, 0.0) for h in range(HEADS)], axis=0)


def _gate_log_decay(lr, wup_ref, bgk_ref):
    z = _dot(lr.astype(BF16), wup_ref[...]) + bgk_ref[...]
    return _log_sigmoid(z) * GATE_SCALE


def _gla_epilogue(o, gate, gng_ref):
    outs = []
    for hd in range(HEADS):
        oh = o[:, hd * DV:(hd + 1) * DV]
        outs.append(_rmsnorm(oh, gng_ref[...]))
    return jnp.concatenate(outs, axis=1) * _silu(gate)


def _out_proj(x, o, yc, wo_ref, fg_ref):
    mix = jnp.concatenate([o, yc], axis=1).astype(BF16)
    out = x + _dot(mix, wo_ref[...])
    return _rmsnorm(out, fg_ref[...])


ITEM_SCHEDULE = (0, 4, 0, 1, 0, 0, 0, 1, 0, 0, 0, 6, 3, 0)
ITEM_SCHEDULE_B = (0, 5, 0, 0, 0, 1, 0, 0, 1, 0, 0, 6, 2, 0)


def _project_items(x_ref, ng_ref, wt_ref, p_ref, pv_ref, slot):
    cache = {}

    def norm():
        x = x_ref[...]
        cache["h"] = _rmsnorm(x, ng_ref[...]).astype(BF16)
        p_ref[slot, :, C_X:C_QK] = x

    def to_p(row, col, width=MXU_COLS):
        def item():
            p_ref[slot, :, col:col + width] = _proj(cache["h"], wt_ref, row, row + width)
        return item

    def to_pv(off):
        def item():
            pv_ref[slot, :, off:off + MXU_COLS] = _proj(
                cache["h"], wt_ref, R_V + off, R_V + off + MXU_COLS).astype(BF16)
        return item

    items = [norm, to_p(R_LR, C_LR, RANK)]
    items += [to_p(R_QK + o, C_QK + o) for o in (0, MXU_COLS)]
    items += [to_pv(o) for o in (0, MXU_COLS)]
    items += [to_p(R_CONV + o, C_CONV + o) for o in range(0, 4 * D_CONV, MXU_COLS)]
    items += [to_p(R_GATE + o, C_GATE + o) for o in (0, MXU_COLS)]
    return items


def _finish_tile(p_ref, pv_ref, slot, wup_ref, bgk_ref, gng_ref, cw_ref, wo_ref, fg_ref,
                 y_ref, cout_ref, sbd_ref, state, tail, emit):
    g = _gate_log_decay(p_ref[slot, :, C_LR:C_LR + RANK], wup_ref, bgk_ref)
    emit()

    rt = _iota((TILE, TILE), 0)
    ct = _iota((TILE, TILE), 1)
    cmask = jnp.where(((rt >> DK_SHIFT) == (ct >> DK_SHIFT)) & (ct <= rt), 1.0, 0.0).astype(BF16)
    b = _masked_sum(cmask, g)
    emit()

    nchunk = TILE // CHUNK
    blast_rows = jnp.concatenate(
        [jnp.broadcast_to(b[c * CHUNK + CHUNK - 1:(c + 1) * CHUNK], (CHUNK, D_QK))
         for c in range(nchunk)], axis=0)
    k_all = p_ref[slot, :, C_QK + D_QK:C_GATE]
    k_d = k_all * jnp.exp(blast_rows - b)
    v_all = pv_ref[slot]
    tok_chunk = _iota((TILE, 2 * DK), 0) >> DK_SHIFT
    incr = []
    for pair in range(HEADS // 2):
        kp = k_d[:, pair * 2 * DK:(pair + 1) * 2 * DK]
        lhs_t = jnp.concatenate(
            [jnp.where(tok_chunk == c, kp, 0.0) for c in range(nchunk)], axis=1).astype(BF16)
        incr.append(_dot_tn(lhs_t, v_all[:, pair * 2 * DV:(pair + 1) * 2 * DV]))

    ar = _iota((HEADS * CHUNK, CHUNK), 0) & (CHUNK - 1)
    ac = _iota((HEADS * CHUNK, CHUNK), 1)
    causal = ac <= ar

    state = list(state)
    o_chunks = []
    for c in range(nchunk):
        r0 = c * CHUNK
        bc = b[r0:r0 + CHUNK]
        qc = p_ref[slot, r0:r0 + CHUNK, C_QK:C_QK + D_QK] * Q_SCALE
        kc = p_ref[slot, r0:r0 + CHUNK, C_QK + D_QK:C_GATE]
        vc = pv_ref[slot, r0:r0 + CHUNK, :]
        bmid = bc[CHUNK // 2:CHUNK // 2 + 1]
        blast = bc[CHUNK - 1:CHUNK]
        q_in = (qc * jnp.exp(bc)).astype(BF16)
        q_a = qc * jnp.exp(bc - bmid)
        k_a = (kc * jnp.exp(bmid - bc)).astype(BF16)

        a_all = _dot_nt(_head_stack(q_a).astype(BF16), k_a)
        a_all = jnp.where(causal, a_all, 0.0).astype(BF16)
        for hd in range(HEADS):
            sbd_ref[c, hd * DK:(hd + 1) * DK, hd * DV:(hd + 1) * DV] = state[hd].astype(BF16)
        inter = _dot(q_in, sbd_ref[c])
        intra = jnp.concatenate(
            [_dot(a_all[hd * CHUNK:(hd + 1) * CHUNK], vc[:, hd * DV:(hd + 1) * DV])
             for hd in range(HEADS)], axis=1)
        o_chunks.append(inter + intra)

        dec = jnp.transpose(jnp.broadcast_to(jnp.exp(blast), (DV, D_QK)))
        for hd in range(HEADS):
            rows = slice(hd * DK, (hd + 1) * DK)
            pair, sub = divmod(hd, 2)
            i0 = c * 2 * DK + sub * DK
            state[hd] = state[hd] * dec[rows] + incr[pair][i0:i0 + DK, sub * DV:(sub + 1) * DV]
        emit()

    o = jnp.concatenate(o_chunks, axis=0)
    mix = []
    for hd in range(HEADS):
        cols = slice(hd * DV, (hd + 1) * DV)
        gate = p_ref[slot, :, C_GATE + hd * DV:C_GATE + (hd + 1) * DV]
        mix.append((_rmsnorm(o[:, cols], gng_ref[...]) * _silu(gate)).astype(BF16))
        emit()

    cw = cw_ref[...]
    row = _iota((TILE, MXU_COLS), 0)
    new_tail = []
    for half in range(D_CONV // MXU_COLS):
        cols = slice(half * MXU_COLS, (half + 1) * MXU_COLS)

        def conv_in(k, cols=cols):
            return p_ref[slot, :, C_CONV + k * D_CONV + cols.start:C_CONV + k * D_CONV + cols.stop]

        hc = conv_in(2) * conv_in(0)
        prev = tail[:, cols]
        h1 = jnp.where(row == 0, prev[7:8], pltpu.roll(hc, 1, 0))
        h2 = jnp.where(row == 0, prev[6:7], jnp.where(row == 1, prev[7:8], pltpu.roll(hc, 2, 0)))
        yc = cw[0:1, cols] * h2 + cw[1:2, cols] * h1 + cw[2:3, cols] * hc
        mix.append((conv_in(1) * yc * _silu(conv_in(3))).astype(BF16))
        new_tail.append(hc[TILE - 8:TILE])
        cout_ref[:, cols] = hc[TILE - 2:TILE]
        emit()

    out = p_ref[slot, :, C_X:C_QK] + _dot(jnp.concatenate(mix, axis=1), wo_ref[...])
    emit()
    half_rows = TILE // 2
    y_ref[0:half_rows] = _rmsnorm(out[0:half_rows], fg_ref[...])
    emit()
    y_ref[half_rows:TILE] = _rmsnorm(out[half_rows:TILE], fg_ref[...])
    return state, jnp.concatenate(new_tail, axis=1)


def _prompt_kernel(pairs_per_seq, xa_ref, xb_ref, ng_ref, wt_ref, wup_ref, bgk_ref, gng_ref, cw_ref,
                   wo_ref, fg_ref,
                   y_ref, sout_ref, cout_ref,
                   p_ref, pv_ref, s_ref, sbd_ref, tail_ref):
    step = pl.program_id(0)
    nchunk = TILE // CHUNK

    @pl.when(step == 0)
    def _():
        p_ref[0] = jnp.zeros(p_ref.shape[1:], F32)
        pv_ref[0] = jnp.zeros(pv_ref.shape[1:], BF16)
        sbd_ref[...] = jnp.zeros_like(sbd_ref)

    @pl.when(jnp.logical_or(step == 0, (step - 1) % pairs_per_seq == 0))
    def _():
        s_ref[...] = jnp.zeros_like(s_ref)
        tail_ref[...] = jnp.zeros_like(tail_ref)

    def half(x_ref, write_slot, read_slot, y_rows, sbd_slabs, state, tail, schedule):
        items = _project_items(x_ref, ng_ref, wt_ref, p_ref, pv_ref, write_slot)
        counts = iter(schedule)
        items[0]()
        pending = iter(items[1:])

        def emit():
            for _ in range(next(counts)):
                next(pending)()

        out = _finish_tile(p_ref, pv_ref, read_slot, wup_ref, bgk_ref, gng_ref, cw_ref, wo_ref, fg_ref,
                           y_ref.at[y_rows], cout_ref, sbd_ref.at[sbd_slabs], state, tail, emit)
        assert next(counts, None) is None and next(pending, None) is None
        return out

    state = [s_ref[hd] for hd in range(HEADS)]
    tail = tail_ref[...]
    state, tail = half(xa_ref, 1, 0, pl.ds(0, TILE), pl.ds(0, nchunk), state, tail, ITEM_SCHEDULE)
    state, tail = half(xb_ref, 0, 1, pl.ds(TILE, TILE), pl.ds(nchunk, nchunk), state, tail, ITEM_SCHEDULE_B)
    for hd in range(HEADS):
        s_ref[hd] = state[hd]
        sout_ref[hd] = state[hd]
    tail_ref[...] = tail


def _sample_kernel(x_ref, cprev_ref, sin_ref, ng_ref, w_ref, wup_ref, bgk_ref,
                   gng_ref, cw_ref, wo_ref, fg_ref,
                   y_ref, sout_ref, cout_ref,
                   q4_ref, inter_ref):
    nt = SEQ_BLK * DEC_LEN
    gt = GRP * DEC_LEN
    n_grp = SEQ_BLK // GRP

    def seqs(g):
        return slice(g * GRP, (g + 1) * GRP)

    x = jnp.concatenate([x_ref[seqs(g), t, :] for g in range(n_grp) for t in range(DEC_LEN)], axis=0)
    h = _rmsnorm(x, ng_ref[...]).astype(BF16)

    qk = _proj(h, w_ref, R_QK, R_V)
    v = _proj(h, w_ref, R_V, R_GATE).astype(BF16)
    gate = _proj(h, w_ref, R_GATE, R_CONV)
    g = _gate_log_decay(_proj(h, w_ref, R_LR, R_LR + RANK), wup_ref, bgk_ref)

    def tok(i):
        return (i >> GRP_SHIFT) & (DEC_LEN - 1)

    def same_seq(r, c):
        return ((r >> GT_SHIFT) == (c >> GT_SHIFT)) & ((r & (GRP - 1)) == (c & (GRP - 1)))

    rt = _iota((nt, nt), 0)
    ct = _iota((nt, nt), 1)
    same = same_seq(rt, ct)
    cmask = jnp.where(same & (tok(ct) <= tok(rt)), 1.0, 0.0).astype(BF16)
    fmask = jnp.where(same, 1.0, 0.0).astype(BF16)
    b = _masked_sum(cmask, g)
    bl = _masked_sum(fmask, g)

    q = qk[:, 0:D_QK] * Q_SCALE
    k = qk[:, D_QK:2 * D_QK]
    q_in = (q * jnp.exp(b)).astype(BF16)
    k_a = (k * jnp.exp(-b)).astype(BF16)
    k_d = k * jnp.exp(bl - b)
    k_dt = jnp.transpose(k_d).astype(BF16)
    dec_t = jnp.transpose(jnp.exp(bl))

    ar = _iota((HEADS * gt, gt), 0) & (gt - 1)
    ac = _iota((HEADS * gt, gt), 1)
    amask = same_seq(ar, ac) & (tok(ac) <= tok(ar))
    for gi in range(n_grp):
        stacked = _head_stack((q * jnp.exp(b))[gi * gt:(gi + 1) * gt])
        for lb in range(D_QK // LANES):
            q4_ref[gi, lb] = stacked[:, lb * LANES:(lb + 1) * LANES]
    for j in range(SEQ_BLK):
        gi, sl = divmod(j, GRP)
        lhs = jnp.concatenate(
            [q4_ref[gi, lb, pl.ds(sl, HEADS * DEC_LEN, stride=GRP), :] for lb in range(D_QK // LANES)],
            axis=1).astype(BF16)
        res = _dot(lhs, sin_ref[j].reshape(D_QK, DV).astype(BF16))
        for hd in range(HEADS):
            inter_ref[hd, pl.ds(gi * gt + sl, DEC_LEN, stride=GRP), :] = res[hd * DEC_LEN:(hd + 1) * DEC_LEN]

    o_groups = []
    for gi in range(n_grp):
        r0 = gi * gt
        qg = q_in[r0:r0 + gt]
        a_all = _dot_nt(_head_stack(qg), k_a[r0:r0 + gt])
        a_all = jnp.where(amask, a_all, 0.0).astype(BF16)
        outs = []
        for hd in range(HEADS):
            rows = slice(hd * gt, (hd + 1) * gt)
            outs.append(_dot(a_all[rows], v[r0:r0 + gt, hd * DV:(hd + 1) * DV]))
        o_groups.append(jnp.concatenate(outs, axis=1))
    o = jnp.concatenate([inter_ref[hd] for hd in range(HEADS)], axis=1) + jnp.concatenate(o_groups, axis=0)
    o = _gla_epilogue(o, gate, gng_ref)

    ur = _iota((SEQ_BLK * DK, nt), 0) >> DK_SHIFT
    uc = _iota((SEQ_BLK * DK, nt), 1)
    umask = ur == (((uc >> GT_SHIFT) << GRP_SHIFT) | (uc & (GRP - 1)))
    for hd in range(HEADS):
        kt = jnp.tile(k_dt[hd * DK:(hd + 1) * DK], (SEQ_BLK, 1))
        u_h = _dot(jnp.where(umask, kt, 0.0), v[:, hd * DV:(hd + 1) * DV])
        for j in range(SEQ_BLK):
            col = (j // GRP) * gt + j % GRP
            dec = jnp.broadcast_to(dec_t[hd * DK:(hd + 1) * DK, col:col + 1], (DK, DV))
            sout_ref[j, hd] = sin_ref[j, hd] * dec + u_h[j * DK:(j + 1) * DK]

    conv = _proj(h, w_ref, R_CONV, R_LR)
    hc = conv[:, 2 * D_CONV:3 * D_CONV] * conv[:, 0:D_CONV]
    h1, h2 = [], []
    for gi in range(n_grp):
        r0 = gi * gt
        c0 = cprev_ref[seqs(gi), 0, :]
        c1 = cprev_ref[seqs(gi), 1, :]
        h1 += [c1, hc[r0:r0 + gt - GRP]]
        h2 += [c0, c1, hc[r0:r0 + gt - 2 * GRP]]
        cout_ref[seqs(gi), 0, :] = hc[r0 + gt - 2 * GRP:r0 + gt - GRP]
        cout_ref[seqs(gi), 1, :] = hc[r0 + gt - GRP:r0 + gt]
    cw = cw_ref[...]
    yc = cw[0:1] * jnp.concatenate(h2, axis=0) + cw[1:2] * jnp.concatenate(h1, axis=0) + cw[2:3] * hc
    yc = conv[:, D_CONV:2 * D_CONV] * yc * _silu(conv[:, 3 * D_CONV:4 * D_CONV])

    y = _out_proj(x, o, yc, wo_ref, fg_ref)
    for gi in range(n_grp):
        for t in range(DEC_LEN):
            r0 = gi * gt + t * GRP
            y_ref[seqs(gi), t, :] = y[r0:r0 + GRP]


def _const_spec(shape):
    return pl.BlockSpec(shape, lambda *_: (0,) * len(shape))


def _weight_specs():
    return [
        _const_spec((1, D_MODEL)),
        _const_spec((D_MODEL, N_W)),
        _const_spec((RANK, D_QK)),
        _const_spec((1, D_QK)),
        _const_spec((1, DV)),
        _const_spec((3, D_CONV)),
        _const_spec((D_MODEL, D_MODEL)),
        _const_spec((1, D_MODEL)),
    ]


def _prep_kernel(wt_ref, wo_ref, w_ref, wob_ref):
    w_ref[...] = jnp.transpose(wt_ref[...]).astype(BF16)
    wob_ref[...] = wo_ref[...].astype(BF16)


def _prepare_weights(w_in, w_out):
    n_qkvg = SRC_LR // PREP_BLK
    n_conv = (4 * D_CONV) // PREP_BLK
    n_blk = N_W // PREP_BLK
    assert n_blk == n_qkvg + n_conv + 1

    def src_row(i):
        row = jnp.where(i < n_qkvg, PREP_BLK * i,
                        jnp.where(i < n_qkvg + n_conv, SRC_CONV + PREP_BLK * (i - n_qkvg), SRC_LR))
        return pl.multiple_of(row, RANK)

    wo_rows = D_MODEL // n_blk
    return pl.pallas_call(
        _prep_kernel,
        grid=(n_blk,),
        in_specs=[
            pl.BlockSpec((pl.Element(PREP_BLK), pl.Element(D_MODEL)), lambda i: (src_row(i), 0)),
            pl.BlockSpec((wo_rows, D_MODEL), lambda i: (i, 0)),
        ],
        out_specs=[
            pl.BlockSpec((D_MODEL, PREP_BLK), lambda i: (0, i)),
            pl.BlockSpec((wo_rows, D_MODEL), lambda i: (i, 0)),
        ],
        out_shape=[
            jax.ShapeDtypeStruct((D_MODEL, N_W), BF16),
            jax.ShapeDtypeStruct((D_MODEL, D_MODEL), BF16),
        ],
        compiler_params=pltpu.CompilerParams(dimension_semantics=("arbitrary",)),
        name="weight_prep",
    )(jnp.swapaxes(w_in[0], 0, 1), w_out[0])


def kernel(x_prompt, x_sample, state_gla, state_conv, norm_gain, w_in, w_gk_up, b_gk,
           gla_norm_gain, conv_w, w_out, final_norm_gain):
    n_batch, seq_len, _ = x_prompt.shape
    n_dec = x_sample.shape[0]
    tiles_per_seq = seq_len // TILE
    n_tiles = n_batch * tiles_per_seq
    w_proj, w_o = _prepare_weights(w_in, w_out)
    weights = (
        norm_gain.reshape(1, D_MODEL),
        w_proj,
        w_gk_up[0].astype(BF16),
        b_gk.reshape(1, D_QK),
        gla_norm_gain.reshape(1, DV),
        conv_w[0],
        w_o,
        final_norm_gain.reshape(1, D_MODEL),
    )

    pairs_per_seq = tiles_per_seq // 2

    def tile_index(t):
        t = jnp.clip(t, 0, n_tiles - 1)
        return (t // tiles_per_seq, t % tiles_per_seq, 0)

    def out_pair(j):
        p = jnp.maximum(j - 1, 0)
        return (p // pairs_per_seq, p % pairs_per_seq, 0)

    def out_seq(j):
        return jnp.maximum(j - 1, 0) // pairs_per_seq

    y_p, s_p, c_p = pl.pallas_call(
        functools.partial(_prompt_kernel, pairs_per_seq),
        grid=(n_tiles // 2 + 1,),
        in_specs=[pl.BlockSpec((None, TILE, D_MODEL), lambda j: tile_index(2 * j - 1)),
                  pl.BlockSpec((None, TILE, D_MODEL), lambda j: tile_index(2 * j))] + _weight_specs(),
        out_specs=[
            pl.BlockSpec((None, 2 * TILE, D_MODEL), out_pair),
            pl.BlockSpec((None, None, HEADS, DK, DV), lambda j: (0, out_seq(j), 0, 0, 0)),
            pl.BlockSpec((None, None, 2, D_CONV), lambda j: (0, out_seq(j), 0, 0)),
        ],
        out_shape=[
            jax.ShapeDtypeStruct((n_batch, seq_len, D_MODEL), F32),
            jax.ShapeDtypeStruct((1, n_batch, HEADS, DK, DV), F32),
            jax.ShapeDtypeStruct((1, n_batch, 2, D_CONV), F32),
        ],
        scratch_shapes=[
            pltpu.VMEM((2, TILE, C_END), F32),
            pltpu.VMEM((2, TILE, D_GLA), BF16),
            pltpu.VMEM((HEADS, DK, DV), F32),
            pltpu.VMEM((2 * (TILE // CHUNK), D_QK, D_GLA), BF16),
            pltpu.VMEM((8, D_CONV), F32),
        ],
        compiler_params=pltpu.CompilerParams(
            dimension_semantics=("arbitrary",), vmem_limit_bytes=VMEM_LIMIT),
        name="gla_conv_prompt",
    )(x_prompt, x_prompt, *weights)

    y_s, s_s, c_s = pl.pallas_call(
        _sample_kernel,
        grid=(n_dec // SEQ_BLK,),
        in_specs=[
            pl.BlockSpec((SEQ_BLK, DEC_LEN, D_MODEL), lambda i: (i, 0, 0)),
            pl.BlockSpec((None, SEQ_BLK, 2, D_CONV), lambda i: (0, i, 0, 0)),
            pl.BlockSpec((None, SEQ_BLK, HEADS, DK, DV), lambda i: (0, i, 0, 0, 0)),
        ] + _weight_specs(),
        out_specs=[
            pl.BlockSpec((SEQ_BLK, DEC_LEN, D_MODEL), lambda i: (i, 0, 0)),
            pl.BlockSpec((None, SEQ_BLK, HEADS, DK, DV), lambda i: (0, i, 0, 0, 0)),
            pl.BlockSpec((None, SEQ_BLK, 2, D_CONV), lambda i: (0, i, 0, 0)),
        ],
        out_shape=[
            jax.ShapeDtypeStruct((n_dec, DEC_LEN, D_MODEL), F32),
            jax.ShapeDtypeStruct((1, n_dec, HEADS, DK, DV), F32),
            jax.ShapeDtypeStruct((1, n_dec, 2, D_CONV), F32),
        ],
        scratch_shapes=[
            pltpu.VMEM((SEQ_BLK // GRP, D_QK // LANES, HEADS * GRP * DEC_LEN, LANES), F32),
            pltpu.VMEM((HEADS, SEQ_BLK * DEC_LEN, DV), F32),
        ],
        compiler_params=pltpu.CompilerParams(
            dimension_semantics=("arbitrary",), vmem_limit_bytes=VMEM_LIMIT),
        name="gla_conv_sample",
    )(x_sample, state_conv, state_gla, *weights)

    return (y_p, y_s, s_p, c_p, s_s, c_s)
```

```python
import functools

import jax
import jax.numpy as jnp
from jax import lax
from jax.experimental import pallas as pl
from jax.experimental.pallas import tpu as pltpu

D_MODEL = 1024
HEADS = 4
DK = 64
DV = 128
D_QK = HEADS * DK
D_GLA = HEADS * DV
D_CONV = 512
RANK = 16
CHUNK = 64
TILE = 256
DEC_LEN = 4
SEQ_BLK = 32
GRP = 16
LANES = 128
MXU_COLS = 256
DK_SHIFT = 6
GRP_SHIFT = 4
GT_SHIFT = 6
EPS = 1e-6
Q_SCALE = DK ** -0.5
GATE_SCALE = 1.0 / 16.0
VMEM_LIMIT = 52 * 1024 * 1024

R_QK = 0
R_V = 2 * D_QK
R_GATE = R_V + D_GLA
R_CONV = R_GATE + D_GLA
R_LR = R_CONV + 4 * D_CONV
PREP_BLK = 512
N_W = R_LR + PREP_BLK
SRC_LR = 2 * D_QK + 2 * D_GLA
SRC_CONV = SRC_LR + RANK

C_X = 0
C_QK = C_X + D_MODEL
C_GATE = C_QK + 2 * D_QK
C_CONV = C_GATE + D_GLA
C_LR = C_CONV + 4 * D_CONV
C_END = C_LR + LANES

F32 = jnp.float32
BF16 = jnp.bfloat16


def _dot(a, b):
    return jnp.dot(a, b, preferred_element_type=F32)


def _dot_nt(a, b):
    return lax.dot_general(a, b, (((1,), (1,)), ((), ())), preferred_element_type=F32)


def _proj(h, w_ref, lo, hi):
    return _dot(h, w_ref[:, lo:hi])


def _dot_tn(a, b):
    return lax.dot_general(a, b, (((0,), (0,)), ((), ())), preferred_element_type=F32)


def _rmsnorm(x, gain):
    ms = jnp.mean(x * x, axis=-1, keepdims=True)
    return x * lax.rsqrt(ms + EPS) * gain


def _silu(x):
    return x * (1.0 / (1.0 + jnp.exp(-x)))


def _log_sigmoid(z):
    return jnp.minimum(z, 0.0) - jnp.log(1.0 + jnp.exp(-jnp.abs(z)))


def _iota(shape, dim):
    return lax.broadcasted_iota(jnp.int32, shape, dim)


def _masked_sum(mask_bf16, g):
    g1 = g.astype(BF16)
    r1 = g - g1.astype(F32)
    g2 = r1.astype(BF16)
    g3 = (r1 - g2.astype(F32)).astype(BF16)
    return _dot(mask_bf16, g1) + _dot(mask_bf16, g2) + _dot(mask_bf16, g3)


def _head_stack(q):
    lane_head = _iota(q.shape, 1) >> DK_SHIFT
    return jnp.concatenate([jnp.where(lane_head == h, q, 0.0) for h in range(HEADS)], axis=0)


def _gate_log_decay(lr, wup_ref, bgk_ref):
    z = _dot(lr.astype(BF16), wup_ref[...]) + bgk_ref[...]
    return _log_sigmoid(z) * GATE_SCALE


def _gla_epilogue(o, gate, gng_ref):
    outs = []
    for hd in range(HEADS):
        oh = o[:, hd * DV:(hd + 1) * DV]
        outs.append(_rmsnorm(oh, gng_ref[...]))
    return jnp.concatenate(outs, axis=1) * _silu(gate)


def _out_proj(x, o, yc, wo_ref, fg_ref):
    mix = jnp.concatenate([o, yc], axis=1).astype(BF16)
    out = x + _dot(mix, wo_ref[...])
    return _rmsnorm(out, fg_ref[...])


ITEM_SCHEDULE = (0, 4, 0, 1, 0, 0, 0, 1, 0, 0, 0, 6, 3, 0)
ITEM_SCHEDULE_B = (0, 5, 0, 0, 0, 1, 0, 0, 1, 0, 0, 6, 2, 0)


def _project_items(x_ref, ng_ref, wt_ref, p_ref, pv_ref, slot):
    cache = {}

    def norm():
        x = x_ref[...]
        cache["h"] = _rmsnorm(x, ng_ref[...]).astype(BF16)
        p_ref[slot, :, C_X:C_QK] = x

    def to_p(row, col, width=MXU_COLS):
        def item():
            p_ref[slot, :, col:col + width] = _proj(cache["h"], wt_ref, row, row + width)
        return item

    def to_pv(off):
        def item():
            pv_ref[slot, :, off:off + MXU_COLS] = _proj(
                cache["h"], wt_ref, R_V + off, R_V + off + MXU_COLS).astype(BF16)
        return item

    items = [norm, to_p(R_LR, C_LR, RANK)]
    items += [to_p(R_CONV + o, C_CONV + o) for o in range(0, 4 * D_CONV, MXU_COLS)]
    items += [to_p(R_QK + o, C_QK + o) for o in (0, MXU_COLS)]
    items += [to_pv(o) for o in (0, MXU_COLS)]
    items += [to_p(R_GATE + o, C_GATE + o) for o in (0, MXU_COLS)]
    return items


def _finish_tile(p_ref, pv_ref, slot, wup_ref, bgk_ref, gng_ref, cw_ref, wo_ref, fg_ref,
                 y_ref, cout_ref, sbd_ref, state, tail, emit):
    g = _gate_log_decay(p_ref[slot, :, C_LR:C_LR + RANK], wup_ref, bgk_ref)
    emit()

    rt = _iota((TILE, TILE), 0)
    ct = _iota((TILE, TILE), 1)
    cmask = jnp.where(((rt >> DK_SHIFT) == (ct >> DK_SHIFT)) & (ct <= rt), 1.0, 0.0).astype(BF16)
    b = _masked_sum(cmask, g)
    emit()

    nchunk = TILE // CHUNK
    blast_rows = jnp.concatenate(
        [jnp.broadcast_to(b[c * CHUNK + CHUNK - 1:(c + 1) * CHUNK], (CHUNK, D_QK))
         for c in range(nchunk)], axis=0)
    k_all = p_ref[slot, :, C_QK + D_QK:C_GATE]
    k_d = k_all * jnp.exp(blast_rows - b)
    v_all = pv_ref[slot]
    tok_chunk = _iota((TILE, 2 * DK), 0) >> DK_SHIFT
    incr = []
    for pair in range(HEADS // 2):
        kp = k_d[:, pair * 2 * DK:(pair + 1) * 2 * DK]
        lhs_t = jnp.concatenate(
            [jnp.where(tok_chunk == c, kp, 0.0) for c in range(nchunk)], axis=1).astype(BF16)
        incr.append(_dot_tn(lhs_t, v_all[:, pair * 2 * DV:(pair + 1) * 2 * DV]))

    ar = _iota((HEADS * CHUNK, CHUNK), 0) & (CHUNK - 1)
    ac = _iota((HEADS * CHUNK, CHUNK), 1)
    causal = ac <= ar

    state = list(state)
    o_chunks = []
    for c in range(nchunk):
        r0 = c * CHUNK
        bc = b[r0:r0 + CHUNK]
        qc = p_ref[slot, r0:r0 + CHUNK, C_QK:C_QK + D_QK] * Q_SCALE
        kc = p_ref[slot, r0:r0 + CHUNK, C_QK + D_QK:C_GATE]
        vc = pv_ref[slot, r0:r0 + CHUNK, :]
        bmid = bc[CHUNK // 2:CHUNK // 2 + 1]
        blast = bc[CHUNK - 1:CHUNK]
        q_in = (qc * jnp.exp(bc)).astype(BF16)
        q_a = qc * jnp.exp(bc - bmid)
        k_a = (kc * jnp.exp(bmid - bc)).astype(BF16)

        a_all = _dot_nt(_head_stack(q_a).astype(BF16), k_a)
        a_all = jnp.where(causal, a_all, 0.0).astype(BF16)
        for hd in range(HEADS):
            sbd_ref[c, hd * DK:(hd + 1) * DK, hd * DV:(hd + 1) * DV] = state[hd].astype(BF16)
        inter = _dot(q_in, sbd_ref[c])
        intra = jnp.concatenate(
            [_dot(a_all[hd * CHUNK:(hd + 1) * CHUNK], vc[:, hd * DV:(hd + 1) * DV])
             for hd in range(HEADS)], axis=1)
        o_chunks.append(inter + intra)

        dec = jnp.transpose(jnp.broadcast_to(jnp.exp(blast), (DV, D_QK)))
        for hd in range(HEADS):
            rows = slice(hd * DK, (hd + 1) * DK)
            pair, sub = divmod(hd, 2)
            i0 = c * 2 * DK + sub * DK
            state[hd] = state[hd] * dec[rows] + incr[pair][i0:i0 + DK, sub * DV:(sub + 1) * DV]
        emit()

    o = jnp.concatenate(o_chunks, axis=0)
    mix = []
    for hd in range(HEADS):
        cols = slice(hd * DV, (hd + 1) * DV)
        gate = p_ref[slot, :, C_GATE + hd * DV:C_GATE + (hd + 1) * DV]
        mix.append((_rmsnorm(o[:, cols], gng_ref[...]) * _silu(gate)).astype(BF16))
        emit()

    cw = cw_ref[...]
    row = _iota((TILE, MXU_COLS), 0)
    new_tail = []
    for half in range(D_CONV // MXU_COLS):
        cols = slice(half * MXU_COLS, (half + 1) * MXU_COLS)

        def conv_in(k, cols=cols):
            return p_ref[slot, :, C_CONV + k * D_CONV + cols.start:C_CONV + k * D_CONV + cols.stop]

        hc = conv_in(2) * conv_in(0)
        prev = tail[:, cols]
        h1 = jnp.where(row == 0, prev[7:8], pltpu.roll(hc, 1, 0))
        h2 = jnp.where(row == 0, prev[6:7], jnp.where(row == 1, prev[7:8], pltpu.roll(hc, 2, 0)))
        yc = cw[0:1, cols] * h2 + cw[1:2, cols] * h1 + cw[2:3, cols] * hc
        mix.append((conv_in(1) * yc * _silu(conv_in(3))).astype(BF16))
        new_tail.append(hc[TILE - 8:TILE])
        cout_ref[:, cols] = hc[TILE - 2:TILE]
        emit()

    out = p_ref[slot, :, C_X:C_QK] + _dot(jnp.concatenate(mix, axis=1), wo_ref[...])
    emit()
    half_rows = TILE // 2
    y_ref[0:half_rows] = _rmsnorm(out[0:half_rows], fg_ref[...])
    emit()
    y_ref[half_rows:TILE] = _rmsnorm(out[half_rows:TILE], fg_ref[...])
    return state, jnp.concatenate(new_tail, axis=1)


def _prompt_kernel(pairs_per_seq, xa_ref, xb_ref, ng_ref, wt_ref, wup_ref, bgk_ref, gng_ref, cw_ref,
                   wo_ref, fg_ref,
                   y_ref, sout_ref, cout_ref,
                   p_ref, pv_ref, s_ref, sbd_ref, tail_ref):
    step = pl.program_id(0)
    nchunk = TILE // CHUNK

    @pl.when(step == 0)
    def _():
        p_ref[0] = jnp.zeros(p_ref.shape[1:], F32)
        pv_ref[0] = jnp.zeros(pv_ref.shape[1:], BF16)
        sbd_ref[...] = jnp.zeros_like(sbd_ref)

    @pl.when(jnp.logical_or(step == 0, (step - 1) % pairs_per_seq == 0))
    def _():
        s_ref[...] = jnp.zeros_like(s_ref)
        tail_ref[...] = jnp.zeros_like(tail_ref)

    def half(x_ref, write_slot, read_slot, y_rows, sbd_slabs, state, tail, schedule):
        items = _project_items(x_ref, ng_ref, wt_ref, p_ref, pv_ref, write_slot)
        counts = iter(schedule)
        items[0]()
        pending = iter(items[1:])

        def emit():
            for _ in range(next(counts)):
                next(pending)()

        out = _finish_tile(p_ref, pv_ref, read_slot, wup_ref, bgk_ref, gng_ref, cw_ref, wo_ref, fg_ref,
                           y_ref.at[y_rows], cout_ref, sbd_ref.at[sbd_slabs], state, tail, emit)
        assert next(counts, None) is None and next(pending, None) is None
        return out

    state = [s_ref[hd] for hd in range(HEADS)]
    tail = tail_ref[...]
    state, tail = half(xa_ref, 1, 0, pl.ds(0, TILE), pl.ds(0, nchunk), state, tail, ITEM_SCHEDULE)
    state, tail = half(xb_ref, 0, 1, pl.ds(TILE, TILE), pl.ds(nchunk, nchunk), state, tail, ITEM_SCHEDULE_B)
    for hd in range(HEADS):
        s_ref[hd] = state[hd]
        sout_ref[hd] = state[hd]
    tail_ref[...] = tail


def _sample_kernel(x_ref, cprev_ref, sin_ref, ng_ref, w_ref, wup_ref, bgk_ref,
                   gng_ref, cw_ref, wo_ref, fg_ref,
                   y_ref, sout_ref, cout_ref,
                   q4_ref, inter_ref):
    nt = SEQ_BLK * DEC_LEN
    gt = GRP * DEC_LEN
    n_grp = SEQ_BLK // GRP

    def seqs(g):
        return slice(g * GRP, (g + 1) * GRP)

    x = jnp.concatenate([x_ref[seqs(g), t, :] for g in range(n_grp) for t in range(DEC_LEN)], axis=0)
    h = _rmsnorm(x, ng_ref[...]).astype(BF16)

    qk = _proj(h, w_ref, R_QK, R_V)
    v = _proj(h, w_ref, R_V, R_GATE).astype(BF16)
    gate = _proj(h, w_ref, R_GATE, R_CONV)
    g = _gate_log_decay(_proj(h, w_ref, R_LR, R_LR + RANK), wup_ref, bgk_ref)

    def tok(i):
        return (i >> GRP_SHIFT) & (DEC_LEN - 1)

    def same_seq(r, c):
        return ((r >> GT_SHIFT) == (c >> GT_SHIFT)) & ((r & (GRP - 1)) == (c & (GRP - 1)))

    rt = _iota((nt, nt), 0)
    ct = _iota((nt, nt), 1)
    same = same_seq(rt, ct)
    cmask = jnp.where(same & (tok(ct) <= tok(rt)), 1.0, 0.0).astype(BF16)
    fmask = jnp.where(same, 1.0, 0.0).astype(BF16)
    b = _masked_sum(cmask, g)
    bl = _masked_sum(fmask, g)

    q = qk[:, 0:D_QK] * Q_SCALE
    k = qk[:, D_QK:2 * D_QK]
    q_in = (q * jnp.exp(b)).astype(BF16)
    k_a = (k * jnp.exp(-b)).astype(BF16)
    k_d = k * jnp.exp(bl - b)
    k_dt = jnp.transpose(k_d).astype(BF16)
    dec_t = jnp.transpose(jnp.exp(bl))

    ar = _iota((HEADS * gt, gt), 0) & (gt - 1)
    ac = _iota((HEADS * gt, gt), 1)
    amask = same_seq(ar, ac) & (tok(ac) <= tok(ar))
    for gi in range(n_grp):
        stacked = _head_stack((q * jnp.exp(b))[gi * gt:(gi + 1) * gt])
        for lb in range(D_QK // LANES):
            q4_ref[gi, lb] = stacked[:, lb * LANES:(lb + 1) * LANES]
    for j in range(SEQ_BLK):
        gi, sl = divmod(j, GRP)
        lhs = jnp.concatenate(
            [q4_ref[gi, lb, pl.ds(sl, HEADS * DEC_LEN, stride=GRP), :] for lb in range(D_QK // LANES)],
            axis=1).astype(BF16)
        res = _dot(lhs, sin_ref[j].reshape(D_QK, DV).astype(BF16))
        for hd in range(HEADS):
            inter_ref[hd, pl.ds(gi * gt + sl, DEC_LEN, stride=GRP), :] = res[hd * DEC_LEN:(hd + 1) * DEC_LEN]

    o_groups = []
    for gi in range(n_grp):
        r0 = gi * gt
        qg = q_in[r0:r0 + gt]
        a_all = _dot_nt(_head_stack(qg), k_a[r0:r0 + gt])
        a_all = jnp.where(amask, a_all, 0.0).astype(BF16)
        outs = []
        for hd in range(HEADS):
            rows = slice(hd * gt, (hd + 1) * gt)
            outs.append(_dot(a_all[rows], v[r0:r0 + gt, hd * DV:(hd + 1) * DV]))
        o_groups.append(jnp.concatenate(outs, axis=1))
    o = jnp.concatenate([inter_ref[hd] for hd in range(HEADS)], axis=1) + jnp.concatenate(o_groups, axis=0)
    o = _gla_epilogue(o, gate, gng_ref)

    ur = _iota((SEQ_BLK * DK, nt), 0) >> DK_SHIFT
    uc = _iota((SEQ_BLK * DK, nt), 1)
    umask = ur == (((uc >> GT_SHIFT) << GRP_SHIFT) | (uc & (GRP - 1)))
    for hd in range(HEADS):
        kt = jnp.tile(k_dt[hd * DK:(hd + 1) * DK], (SEQ_BLK, 1))
        u_h = _dot(jnp.where(umask, kt, 0.0), v[:, hd * DV:(hd + 1) * DV])
        for j in range(SEQ_BLK):
            col = (j // GRP) * gt + j % GRP
            dec = jnp.broadcast_to(dec_t[hd * DK:(hd + 1) * DK, col:col + 1], (DK, DV))
            sout_ref[j, hd] = sin_ref[j, hd] * dec + u_h[j * DK:(j + 1) * DK]

    conv = _proj(h, w_ref, R_CONV, R_LR)
    hc = conv[:, 2 * D_CONV:3 * D_CONV] * conv[:, 0:D_CONV]
    h1, h2 = [], []
    for gi in range(n_grp):
        r0 = gi * gt
        c0 = cprev_ref[seqs(gi), 0, :]
        c1 = cprev_ref[seqs(gi), 1, :]
        h1 += [c1, hc[r0:r0 + gt - GRP]]
        h2 += [c0, c1, hc[r0:r0 + gt - 2 * GRP]]
        cout_ref[seqs(gi), 0, :] = hc[r0 + gt - 2 * GRP:r0 + gt - GRP]
        cout_ref[seqs(gi), 1, :] = hc[r0 + gt - GRP:r0 + gt]
    cw = cw_ref[...]
    yc = cw[0:1] * jnp.concatenate(h2, axis=0) + cw[1:2] * jnp.concatenate(h1, axis=0) + cw[2:3] * hc
    yc = conv[:, D_CONV:2 * D_CONV] * yc * _silu(conv[:, 3 * D_CONV:4 * D_CONV])

    y = _out_proj(x, o, yc, wo_ref, fg_ref)
    for gi in range(n_grp):
        for t in range(DEC_LEN):
            r0 = gi * gt + t * GRP
            y_ref[seqs(gi), t, :] = y[r0:r0 + GRP]


def _const_spec(shape):
    return pl.BlockSpec(shape, lambda *_: (0,) * len(shape))


def _weight_specs():
    return [
        _const_spec((1, D_MODEL)),
        _const_spec((D_MODEL, N_W)),
        _const_spec((RANK, D_QK)),
        _const_spec((1, D_QK)),
        _const_spec((1, DV)),
        _const_spec((3, D_CONV)),
        _const_spec((D_MODEL, D_MODEL)),
        _const_spec((1, D_MODEL)),
    ]


def _prep_kernel(wt_ref, wo_ref, w_ref, wob_ref):
    w_ref[...] = jnp.transpose(wt_ref[...]).astype(BF16)
    wob_ref[...] = wo_ref[...].astype(BF16)


def _prepare_weights(w_in, w_out):
    n_qkvg = SRC_LR // PREP_BLK
    n_conv = (4 * D_CONV) // PREP_BLK
    n_blk = N_W // PREP_BLK
    assert n_blk == n_qkvg + n_conv + 1

    def src_row(i):
        row = jnp.where(i < n_qkvg, PREP_BLK * i,
                        jnp.where(i < n_qkvg + n_conv, SRC_CONV + PREP_BLK * (i - n_qkvg), SRC_LR))
        return pl.multiple_of(row, RANK)

    wo_rows = D_MODEL // n_blk
    return pl.pallas_call(
        _prep_kernel,
        grid=(n_blk,),
        in_specs=[
            pl.BlockSpec((pl.Element(PREP_BLK), pl.Element(D_MODEL)), lambda i: (src_row(i), 0)),
            pl.BlockSpec((wo_rows, D_MODEL), lambda i: (i, 0)),
        ],
        out_specs=[
            pl.BlockSpec((D_MODEL, PREP_BLK), lambda i: (0, i)),
            pl.BlockSpec((wo_rows, D_MODEL), lambda i: (i, 0)),
        ],
        out_shape=[
            jax.ShapeDtypeStruct((D_MODEL, N_W), BF16),
            jax.ShapeDtypeStruct((D_MODEL, D_MODEL), BF16),
        ],
        compiler_params=pltpu.CompilerParams(dimension_semantics=("arbitrary",)),
        name="weight_prep",
    )(jnp.swapaxes(w_in[0], 0, 1), w_out[0])


def kernel(x_prompt, x_sample, state_gla, state_conv, norm_gain, w_in, w_gk_up, b_gk,
           gla_norm_gain, conv_w, w_out, final_norm_gain):
    n_batch, seq_len, _ = x_prompt.shape
    n_dec = x_sample.shape[0]
    tiles_per_seq = seq_len // TILE
    n_tiles = n_batch * tiles_per_seq
    w_proj, w_o = _prepare_weights(w_in, w_out)
    weights = (
        norm_gain.reshape(1, D_MODEL),
        w_proj,
        w_gk_up[0].astype(BF16),
        b_gk.reshape(1, D_QK),
        gla_norm_gain.reshape(1, DV),
        conv_w[0],
        w_o,
        final_norm_gain.reshape(1, D_MODEL),
    )

    pairs_per_seq = tiles_per_seq // 2

    def tile_index(t):
        t = jnp.clip(t, 0, n_tiles - 1)
        return (t // tiles_per_seq, t % tiles_per_seq, 0)

    def out_pair(j):
        p = jnp.maximum(j - 1, 0)
        return (p // pairs_per_seq, p % pairs_per_seq, 0)

    def out_seq(j):
        return jnp.maximum(j - 1, 0) // pairs_per_seq

    y_p, s_p, c_p = pl.pallas_call(
        functools.partial(_prompt_kernel, pairs_per_seq),
        grid=(n_tiles // 2 + 1,),
        in_specs=[pl.BlockSpec((None, TILE, D_MODEL), lambda j: tile_index(2 * j - 1)),
                  pl.BlockSpec((None, TILE, D_MODEL), lambda j: tile_index(2 * j))] + _weight_specs(),
        out_specs=[
            pl.BlockSpec((None, 2 * TILE, D_MODEL), out_pair),
            pl.BlockSpec((None, None, HEADS, DK, DV), lambda j: (0, out_seq(j), 0, 0, 0)),
            pl.BlockSpec((None, None, 2, D_CONV), lambda j: (0, out_seq(j), 0, 0)),
        ],
        out_shape=[
            jax.ShapeDtypeStruct((n_batch, seq_len, D_MODEL), F32),
            jax.ShapeDtypeStruct((1, n_batch, HEADS, DK, DV), F32),
            jax.ShapeDtypeStruct((1, n_batch, 2, D_CONV), F32),
        ],
        scratch_shapes=[
            pltpu.VMEM((2, TILE, C_END), F32),
            pltpu.VMEM((2, TILE, D_GLA), BF16),
            pltpu.VMEM((HEADS, DK, DV), F32),
            pltpu.VMEM((2 * (TILE // CHUNK), D_QK, D_GLA), BF16),
            pltpu.VMEM((8, D_CONV), F32),
        ],
        compiler_params=pltpu.CompilerParams(
            dimension_semantics=("arbitrary",), vmem_limit_bytes=VMEM_LIMIT),
        name="gla_conv_prompt",
    )(x_prompt, x_prompt, *weights)

    y_s, s_s, c_s = pl.pallas_call(
        _sample_kernel,
        grid=(n_dec // SEQ_BLK,),
        in_specs=[
            pl.BlockSpec((SEQ_BLK, DEC_LEN, D_MODEL), lambda i: (i, 0, 0)),
            pl.BlockSpec((None, SEQ_BLK, 2, D_CONV), lambda i: (0, i, 0, 0)),
            pl.BlockSpec((None, SEQ_BLK, HEADS, DK, DV), lambda i: (0, i, 0, 0, 0)),
        ] + _weight_specs(),
        out_specs=[
            pl.BlockSpec((SEQ_BLK, DEC_LEN, D_MODEL), lambda i: (i, 0, 0)),
            pl.BlockSpec((None, SEQ_BLK, HEADS, DK, DV), lambda i: (0, i, 0, 0, 0)),
            pl.BlockSpec((None, SEQ_BLK, 2, D_CONV), lambda i: (0, i, 0, 0)),
        ],
        out_shape=[
            jax.ShapeDtypeStruct((n_dec, DEC_LEN, D_MODEL), F32),
            jax.ShapeDtypeStruct((1, n_dec, HEADS, DK, DV), F32),
            jax.ShapeDtypeStruct((1, n_dec, 2, D_CONV), F32),
        ],
        scratch_shapes=[
            pltpu.VMEM((SEQ_BLK // GRP, D_QK // LANES, HEADS * GRP * DEC_LEN, LANES), F32),
            pltpu.VMEM((HEADS, SEQ_BLK * DEC_LEN, DV), F32),
        ],
        compiler_params=pltpu.CompilerParams(
            dimension_semantics=("arbitrary",), vmem_limit_bytes=VMEM_LIMIT),
        name="gla_conv_sample",
    )(x_sample, state_conv, state_gla, *weights)

    return (y_p, y_s, s_p, c_p, s_s, c_s)
```
